```python
import jax, jax.numpy as jnp
from jax import lax
import numpy as np

D_MODEL = 2048
BATCH = 4
SEQ = 2048
DEPTH = 4

MIX_WIDTH = D_MODEL
HEAD_DIM = 128
SB_HEADS = 8
SB_WIDTH = SB_HEADS * HEAD_DIM
SB_BLOCK = 128
GLA_HEADS = 4
GLA_HEAD_V = HEAD_DIM
GLA_HEAD_K = HEAD_DIM // 2
GLA_WIDTH = GLA_HEADS * GLA_HEAD_V
GLA_KEY_WIDTH = GLA_HEADS * GLA_HEAD_K
GLA_GATE_RANK = 16
GLA_GATE_TAU = 16.0
GLA_CHUNK = 64
SGU_GROUPS = 4
SGU_WIDTH = MIX_WIDTH - SB_WIDTH - GLA_WIDTH
SGU_GROUP_DIM = SGU_WIDTH // SGU_GROUPS
SGU_CHUNK = 128
N_MIX_HEADS = MIX_WIDTH // HEAD_DIM
N_MEM = 256
XA_HEADS = 4
XA_HEAD_DIM = 128
XA_WIDTH = XA_HEADS * XA_HEAD_DIM
EPS = 1e-6

IN_SPLITS = [SB_WIDTH, SB_WIDTH, SB_WIDTH, SB_WIDTH,
             GLA_KEY_WIDTH, GLA_KEY_WIDTH, GLA_WIDTH, GLA_GATE_RANK, GLA_WIDTH,
             SGU_WIDTH, SGU_WIDTH, SGU_WIDTH]
IN_WIDTH = sum(IN_SPLITS)
IN_OFFSETS = [int(o) for o in np.cumsum(IN_SPLITS)[:-1]]

kernel_name = "hybrid_sb_gla_sgu_parallel_heads"


def rmsnorm(x, g):
    xf = x.astype(jnp.float32)
    y = xf * lax.rsqrt(jnp.mean(xf * xf, axis=-1, keepdims=True) + EPS)
    return (y * g.astype(jnp.float32)).astype(x.dtype)


def to_heads(t, n_heads):
    b, s, _ = t.shape
    return t.reshape(b, s, n_heads, -1).transpose(0, 2, 1, 3)


def from_heads(t):
    b, h, s, d = t.shape
    return t.transpose(0, 2, 1, 3).reshape(b, s, h * d)


def stick_breaking_attention(q, k, v):
    B, H, S, d = q.shape
    nblk = S // SB_BLOCK
    scale = d ** -0.5
    qb = q.reshape(B, H, nblk, SB_BLOCK, d).transpose(2, 0, 1, 3, 4)
    key_pos = jnp.arange(S)

    def block(args):
        q_blk, i = args
        z = jnp.einsum('bhqd,bhkd->bhqk', q_blk, k).astype(jnp.float32) * scale
        q_pos = i * SB_BLOCK + jnp.arange(SB_BLOCK)
        causal = key_pos[None, :] < q_pos[:, None]
        log_keep = jnp.where(causal, jax.nn.log_sigmoid(-z), 0.0)
        suffix = lax.cumsum(log_keep, axis=3, reverse=True) - log_keep
        a = jnp.where(causal, jnp.exp(jax.nn.log_sigmoid(z) + suffix), 0.0)
        return jnp.einsum('bhqk,bhkd->bhqd', a.astype(v.dtype), v)

    out = lax.map(block, (qb, jnp.arange(nblk)))
    return out.transpose(1, 2, 0, 3, 4).reshape(B, H, S, d)


def gla_chunked(q, k, v, log_alpha):
    B, H, S, dk = q.shape
    dv = v.shape[-1]
    C = GLA_CHUNK
    N = S // C
    qc = q.reshape(B, H, N, C, dk).astype(jnp.float32) * (dk ** -0.5)
    kc = k.reshape(B, H, N, C, dk).astype(jnp.float32)
    vc = v.reshape(B, H, N, C, dv).astype(jnp.float32)
    bcum = jnp.cumsum(log_alpha.reshape(B, H, N, C, dk).astype(jnp.float32), axis=3)
    tril = jnp.tril(jnp.ones((C, C), dtype=bool))
    rel = bcum[:, :, :, :, None, :] - bcum[:, :, :, None, :, :]
    decay = jnp.exp(jnp.where(tril[:, :, None], rel, -jnp.inf))
    attn = jnp.einsum('bhntd,bhnsd,bhntsd->bhnts', qc, kc, decay)
    o_intra = jnp.einsum('bhnts,bhnsv->bhntv', attn, vc)
    q_dec = qc * jnp.exp(bcum)
    b_last = bcum[:, :, :, -1:, :]
    k_dec = kc * jnp.exp(b_last - bcum)
    chunk_decay = jnp.exp(b_last[:, :, :, 0, :])
    xs = (jnp.moveaxis(q_dec, 2, 0), jnp.moveaxis(k_dec, 2, 0),
          jnp.moveaxis(vc, 2, 0), jnp.moveaxis(chunk_decay, 2, 0))

    def step(state, inp):
        qn, kn, vn, dn = inp
        o = jnp.einsum('bhtd,bhdv->bhtv', qn, state)
        state = dn[..., None] * state + jnp.einsum('bhsd,bhsv->bhdv', kn, vn)
        return state, o

    state0 = jnp.zeros((B, H, dk, dv), jnp.float32)
    _, o_inter = lax.scan(step, state0, xs)
    o = o_intra + jnp.moveaxis(o_inter, 0, 2)
    return o.reshape(B, H, S, dv).astype(v.dtype)


def chunked_sgu(u, v, g_norm, w_s, b_s):
    B, S, _ = v.shape
    N = S // SGU_CHUNK
    v = rmsnorm(v, g_norm)
    vb = v.reshape(B, N, SGU_CHUNK, SGU_GROUPS, SGU_GROUP_DIM)
    w = w_s * jnp.tril(jnp.ones((SGU_CHUNK, SGU_CHUNK), w_s.dtype))[None]
    mixed = jnp.einsum('gts,bnsgc->bntgc', w, vb) + b_s.T[None, None, :, :, None]
    return u * mixed.reshape(B, S, SGU_WIDTH)


def memory_cross_attention(h, m, w_q, w_kv, w_o):
    B, S, _ = h.shape
    q = (h @ w_q).reshape(B, S, XA_HEADS, XA_HEAD_DIM)
    k, v = jnp.split(m @ w_kv, 2, axis=-1)
    k = k.reshape(B, -1, XA_HEADS, XA_HEAD_DIM)
    v = v.reshape(B, -1, XA_HEADS, XA_HEAD_DIM)
    s = jnp.einsum('bqhd,bkhd->bhqk', q, k).astype(jnp.float32) * (XA_HEAD_DIM ** -0.5)
    p = jax.nn.softmax(s, axis=-1).astype(v.dtype)
    o = jnp.einsum('bhqk,bkhd->bqhd', p, v).reshape(B, S, XA_WIDTH)
    return o @ w_o


def setup_inputs(seed: int = 0) -> dict:
    key = jax.random.key(seed)
    ks = jax.random.split(key, 17)
    f32 = jnp.float32
    nrm = lambda k, shape, s: jax.random.normal(k, shape, f32) * s
    return {
        "x": nrm(ks[0], (BATCH, SEQ, D_MODEL), 1.0),
        "mem": nrm(ks[1], (BATCH, N_MEM, D_MODEL), 1.0),
        "norm_mix": 1.0 + nrm(ks[2], (DEPTH, D_MODEL), 0.02),
        "w_in": nrm(ks[3], (DEPTH, D_MODEL, IN_WIDTH), D_MODEL ** -0.5),
        "w_gla_gate_up": nrm(ks[4], (DEPTH, GLA_GATE_RANK, GLA_KEY_WIDTH), GLA_GATE_RANK ** -0.5),
        "b_gla_gate": nrm(ks[5], (DEPTH, GLA_KEY_WIDTH), 0.1),
        "sgu_norm": 1.0 + nrm(ks[6], (DEPTH, SGU_WIDTH), 0.02),
        "w_sgu": nrm(ks[7], (DEPTH, SGU_GROUPS, SGU_CHUNK, SGU_CHUNK), SGU_CHUNK ** -0.5),
        "b_sgu": 1.0 + nrm(ks[8], (DEPTH, SGU_GROUPS, SGU_CHUNK), 0.1),
        "out_norm": 1.0 + nrm(ks[9], (DEPTH, MIX_WIDTH), 0.02),
        "w_out": nrm(ks[10], (DEPTH, MIX_WIDTH, D_MODEL), MIX_WIDTH ** -0.5),
        "norm_xattn": 1.0 + nrm(ks[11], (DEPTH, D_MODEL), 0.02),
        "norm_mem": 1.0 + nrm(ks[12], (DEPTH, D_MODEL), 0.02),
        "w_xq": nrm(ks[13], (DEPTH, D_MODEL, XA_WIDTH), D_MODEL ** -0.5),
        "w_xkv": nrm(ks[14], (DEPTH, D_MODEL, 2 * XA_WIDTH), D_MODEL ** -0.5),
        "w_xo": nrm(ks[15], (DEPTH, XA_WIDTH, D_MODEL), XA_WIDTH ** -0.5),
        "final_norm": 1.0 + nrm(ks[16], (D_MODEL,), 0.02),
    }


def reference(x, mem, norm_mix, w_in, w_gla_gate_up, b_gla_gate, sgu_norm, w_sgu, b_sgu,
              out_norm, w_out, norm_xattn, norm_mem, w_xq, w_xkv, w_xo, final_norm):
    B, S, _ = x.shape
    for l in range(DEPTH):
        h = rmsnorm(x, norm_mix[l])
        (sb_q, sb_k, sb_v, sb_g,
         gla_q, gla_k, gla_v, gla_r, gla_g,
         sgu_u, sgu_v, sgu_g) = jnp.split(h @ w_in[l], IN_OFFSETS, axis=-1)

        o_sb = from_heads(stick_breaking_attention(
            to_heads(sb_q, SB_HEADS), to_heads(sb_k, SB_HEADS), to_heads(sb_v, SB_HEADS)))

        gate_logits = (gla_r @ w_gla_gate_up[l] + b_gla_gate[l]).astype(jnp.float32)
        log_alpha = jax.nn.log_sigmoid(gate_logits) / GLA_GATE_TAU
        o_gla = from_heads(gla_chunked(
            to_heads(gla_q, GLA_HEADS), to_heads(gla_k, GLA_HEADS),
            to_heads(gla_v, GLA_HEADS), to_heads(log_alpha, GLA_HEADS)))

        o_sgu = chunked_sgu(jax.nn.gelu(sgu_u), jax.nn.gelu(sgu_v), sgu_norm[l], w_sgu[l], b_sgu[l])

        mix = jnp.concatenate([o_sb, o_gla, o_sgu], axis=-1)
        mix = rmsnorm(mix.reshape(B, S, N_MIX_HEADS, HEAD_DIM),
                      out_norm[l].reshape(N_MIX_HEADS, HEAD_DIM)).reshape(B, S, MIX_WIDTH)
        gate = jax.nn.silu(jnp.concatenate([sb_g, gla_g, sgu_g], axis=-1))
        x = x + (mix * gate) @ w_out[l]
        x = x + memory_cross_attention(rmsnorm(x, norm_xattn[l]), rmsnorm(mem, norm_mem[l]),
                                       w_xq[l], w_xkv[l], w_xo[l])
    return rmsnorm(x, final_norm)
```

```python
import functools

import jax
import jax.numpy as jnp
from jax import lax
from jax.experimental import pallas as pl
from jax.experimental.pallas import tpu as pltpu

F32 = jnp.float32
BF16 = jnp.bfloat16

D_MODEL = 2048
BATCH = 4
SEQ = 2048
DEPTH = 4
TOKENS = BATCH * SEQ
HEAD_DIM = 128
SB_HEADS = 8
SB_WIDTH = SB_HEADS * HEAD_DIM
GLA_HEADS = 4
GLA_HEAD_K = 64
GLA_WIDTH = GLA_HEADS * HEAD_DIM
GLA_KEY_WIDTH = GLA_HEADS * GLA_HEAD_K
GLA_GATE_RANK = 16
GLA_GATE_TAU = 16.0
GLA_CHUNK = 64
GLA_SUB = 16
SGU_GROUPS = 4
SGU_WIDTH = 512
SGU_CHUNK = 128
N_MEM = 256
XA_HEADS = 4
XA_WIDTH = XA_HEADS * HEAD_DIM
EPS = 1e-6

LANES = 128

COL_SB_Q = 0
COL_SB_K = SB_WIDTH
COL_SB_V = 2 * SB_WIDTH
COL_SB_G = 3 * SB_WIDTH
COL_GLA_Q = 4 * SB_WIDTH
COL_GLA_K = COL_GLA_Q + GLA_KEY_WIDTH
COL_GLA_V = COL_GLA_K + GLA_KEY_WIDTH
COL_GLA_G = COL_GLA_V + GLA_WIDTH
COL_SGU_U = COL_GLA_G + GLA_WIDTH
COL_SGU_V = COL_SGU_U + SGU_WIDTH
COL_SGU_G = COL_SGU_V + SGU_WIDTH
COL_GLA_R = COL_SGU_G + SGU_WIDTH
PROJ_WIDTH = COL_GLA_R + LANES
ORIG_GLA_R = 4 * SB_WIDTH + 2 * GLA_KEY_WIDTH + GLA_WIDTH

VMEM_LIMIT = 56 * 1024 * 1024


def _params(semantics):
    return pltpu.CompilerParams(dimension_semantics=semantics, vmem_limit_bytes=VMEM_LIMIT)


def _rms(x, g):
    ms = jnp.mean(x * x, axis=-1, keepdims=True)
    return x * lax.rsqrt(ms + EPS) * g


def _silu(g):
    return g / (1.0 + jnp.exp(-g))


def _softplus(z):
    return jnp.maximum(z, 0.0) + jnp.log(1.0 + jnp.exp(-jnp.abs(z)))


def _split_bf16(x):
    hi = x.astype(BF16)
    lo = (x - hi.astype(F32)).astype(BF16)
    return hi, lo


def _gelu_tanh(x):
    return 0.5 * x * (1.0 + jnp.tanh(0.7978845608028654 * (x + 0.044715 * (x * x * x))))


def _norm_kernel(x_ref, g_ref, o_ref):
    o_ref[...] = _rms(x_ref[...], g_ref[...]).astype(o_ref.dtype)


def _norm_rows(x2d, g3d, layer, tm=512):
    rows = x2d.shape[0]
    return pl.pallas_call(
        _norm_kernel,
        grid=(rows // tm,),
        in_specs=[pl.BlockSpec((tm, D_MODEL), lambda i: (i, 0)),
                  pl.BlockSpec((None, 1, D_MODEL), lambda i: (layer, 0, 0))],
        out_specs=pl.BlockSpec((tm, D_MODEL), lambda i: (i, 0)),
        out_shape=jax.ShapeDtypeStruct((rows, D_MODEL), BF16),
        compiler_params=_params(("parallel",)),
        name="rmsnorm_rows",
    )(x2d, g3d)


def _memkv_kernel(m_ref, g_ref, w_ref, o_ref):
    mn = _rms(m_ref[...], g_ref[...]).astype(BF16)
    o_ref[...] = jnp.dot(mn, w_ref[...], preferred_element_type=F32).astype(o_ref.dtype)


def _mem_kv(mem, norm_mem3, w_xkv_bf):
    return pl.pallas_call(
        _memkv_kernel,
        grid=(DEPTH, BATCH),
        in_specs=[pl.BlockSpec((None, N_MEM, D_MODEL), lambda l, b: (b, 0, 0)),
                  pl.BlockSpec((None, 1, D_MODEL), lambda l, b: (l, 0, 0)),
                  pl.BlockSpec((None, D_MODEL, 2 * XA_WIDTH), lambda l, b: (l, 0, 0))],
        out_specs=pl.BlockSpec((None, None, N_MEM, 2 * XA_WIDTH), lambda l, b: (l, b, 0, 0)),
        out_shape=jax.ShapeDtypeStruct((DEPTH, BATCH, N_MEM, 2 * XA_WIDTH), BF16),
        compiler_params=_params(("parallel", "parallel")),
        name="mem_kv",
    )(mem, norm_mem3, w_xkv_bf)


def _proj_kernel(h_ref, w_ref, o_ref):
    o_ref[...] = jnp.dot(h_ref[...], w_ref[...], preferred_element_type=F32).astype(o_ref.dtype)


def _in_proj(h, w_in_p, layer, tm=512, tn=PROJ_WIDTH // 3):
    return pl.pallas_call(
        _proj_kernel,
        grid=(PROJ_WIDTH // tn, TOKENS // tm),
        in_specs=[pl.BlockSpec((tm, D_MODEL), lambda n, m: (m, 0)),
                  pl.BlockSpec((None, D_MODEL, tn), lambda n, m: (layer, 0, n))],
        out_specs=pl.BlockSpec((tm, tn), lambda n, m: (m, n)),
        out_shape=jax.ShapeDtypeStruct((TOKENS, PROJ_WIDTH), BF16),
        compiler_params=_params(("parallel", "parallel")),
        name="in_proj",
    )(h, w_in_p)


SB_TQ = 256
SB_TK = 128


def _sb_kernel(q_ref, k_ref, v_ref, g_ref, tt_ref, gn_ref, o_ref, acc_ref, carry_ref):
    qi = pl.program_id(2)
    q = q_ref[...]
    acc_ref[...] = jnp.zeros_like(acc_ref)
    carry_ref[...] = jnp.zeros_like(carry_ref)
    row = qi * SB_TQ + lax.broadcasted_iota(jnp.int32, (SB_TQ, SB_TK), 0)
    lane = lax.broadcasted_iota(jnp.int32, (SB_TQ, SB_TK), 1)
    nblk = (qi + 1) * (SB_TQ // SB_TK)
    scale = HEAD_DIM ** -0.5

    def body(it, c):
        j = nblk - 1 - it
        start = pl.multiple_of(j * SB_TK, SB_TK)
        k = k_ref[pl.ds(start, SB_TK), :]
        v = v_ref[pl.ds(start, SB_TK), :]
        z = lax.dot_general(q, k, (((1,), (1,)), ((), ())), preferred_element_type=F32) * scale
        sp = _softplus(z)
        mask = (lane + j * SB_TK) < row
        log_keep = jnp.where(mask, -sp, 0.0)
        hi, lo = _split_bf16(log_keep)
        cs = jnp.dot(jnp.concatenate([hi, lo], axis=1), tt_ref[...], preferred_element_type=F32)
        arg = (z - sp) + cs[:, :SB_TK] + carry_ref[...]
        a = jnp.where(mask, jnp.exp(arg), 0.0)
        acc_ref[...] += jnp.dot(a.astype(BF16), v, preferred_element_type=F32)
        carry_ref[...] += cs[:, SB_TK:]
        return c

    lax.fori_loop(0, nblk, body, 0)
    y = _rms(acc_ref[...], gn_ref[...])
    o_ref[...] = (y * _silu(g_ref[...].astype(F32))).astype(o_ref.dtype)


def _sb_attention(proj, tt, out_norm3, layer):
    nq = SEQ // SB_TQ
    qb, kb, vb, gb = (c // HEAD_DIM for c in (COL_SB_Q, COL_SB_K, COL_SB_V, COL_SB_G))
    return pl.pallas_call(
        _sb_kernel,
        grid=(BATCH, SB_HEADS, nq),
        in_specs=[pl.BlockSpec((SB_TQ, HEAD_DIM), lambda b, h, i: (b * nq + i, qb + h)),
                  pl.BlockSpec((SEQ, HEAD_DIM), lambda b, h, i: (b, kb + h)),
                  pl.BlockSpec((SEQ, HEAD_DIM), lambda b, h, i: (b, vb + h)),
                  pl.BlockSpec((SB_TQ, HEAD_DIM), lambda b, h, i: (b * nq + i, gb + h)),
                  pl.BlockSpec((2 * SB_TK, 2 * SB_TK), lambda b, h, i: (0, 0)),
                  pl.BlockSpec((None, 1, HEAD_DIM), lambda b, h, i: (layer, 0, h))],
        out_specs=pl.BlockSpec((SB_TQ, HEAD_DIM), lambda b, h, i: (b * nq + i, h)),
        out_shape=jax.ShapeDtypeStruct((TOKENS, SB_WIDTH), BF16),
        scratch_shapes=[pltpu.VMEM((SB_TQ, HEAD_DIM), F32), pltpu.VMEM((SB_TQ, SB_TK), F32)],
        compiler_params=_params(("parallel", "parallel", "arbitrary")),
        name="sb_attention",
    )(proj, proj, proj, proj, tt, out_norm3)


GLA_RB = 256
GLA_PAD = GLA_SUB
GLA_PAIR = 2 * GLA_HEAD_K


def _gla_kernel(q_ref, k_ref, v_ref, r_ref, g_ref, wup_ref, bup_ref, tril_ref, e_ref, bm_ref, gn_ref,
                o_ref, st_ref, ksh_ref, bsh_ref, vsh_ref):
    @pl.when(pl.program_id(1) == 0)
    def _():
        st_ref[...] = jnp.zeros_like(st_ref)

    ksh_ref[0:GLA_PAD, :] = jnp.zeros((GLA_PAD, GLA_KEY_WIDTH), F32)
    bsh_ref[0:GLA_PAD, :] = jnp.zeros((GLA_PAD, GLA_KEY_WIDTH), F32)
    vsh_ref[0:GLA_PAD, :] = jnp.zeros((GLA_PAD, GLA_WIDTH), F32)

    C = GLA_CHUNK
    lane_k = lax.broadcasted_iota(jnp.int32, (1, GLA_KEY_WIDTH), 1)
    head_masks = [(lane_k // GLA_HEAD_K == h).astype(F32) for h in range(GLA_HEADS)]
    tmod = lax.broadcasted_iota(jnp.int32, (C, GLA_KEY_WIDTH), 0) % GLA_SUB
    col_s = lax.broadcasted_iota(jnp.int32, (C, C), 1)
    nt = (((1,), (1,)), ((), ()))
    tn = (((0,), (0,)), ((), ()))

    def chunk(c, carry):
        r0 = pl.multiple_of(c * C, C)
        rows = pl.ds(r0, C)
        q = q_ref[rows, :].astype(F32) * (GLA_HEAD_K ** -0.5)
        k = k_ref[rows, :].astype(F32)
        v = v_ref[rows, :]
        vf = v.astype(F32)
        logits = jnp.dot(r_ref[rows, :], wup_ref[...], preferred_element_type=F32) + bup_ref[...]
        log_alpha = (jnp.minimum(logits, 0.0) - jnp.log(1.0 + jnp.exp(-jnp.abs(logits)))) * (1.0 / GLA_GATE_TAU)
        hi, lo = _split_bf16(log_alpha)
        tril = tril_ref[...]
        bc = (jnp.dot(tril, hi, preferred_element_type=F32)
              + jnp.dot(tril, lo, preferred_element_type=F32))
        b_last = bc[C - 1:C, :]

        q_dec = (q * jnp.exp(bc)).astype(BF16)
        k_dec = (k * jnp.exp(b_last - bc)).astype(BF16)
        chunk_decay = jnp.exp(b_last)
        o_parts = []
        for p in range(GLA_HEADS // 2):
            ks = slice(p * GLA_PAIR, (p + 1) * GLA_PAIR)
            vs = slice(p * 2 * HEAD_DIM, (p + 1) * 2 * HEAD_DIM)
            st = st_ref[p]
            o_parts.append(lax.dot_general(q_dec[:, ks], st.astype(BF16), nt, preferred_element_type=F32))
            upd = lax.dot_general(v[:, vs], k_dec[:, ks], tn, preferred_element_type=F32)
            st_ref[p] = st * chunk_decay[:, ks] + upd * bm_ref[...]
        o = jnp.concatenate(o_parts, axis=1)

        a_rows = [jnp.zeros((GLA_HEADS * GLA_SUB, C), F32)]
        for i in range(1, C // GLA_SUB):
            lo_r, hi_r = i * GLA_SUB, (i + 1) * GLA_SUB
            b_ref_row = bc[lo_r - 1:lo_r, :]
            q_i = q[lo_r:hi_r, :] * jnp.exp(bc[lo_r:hi_r, :] - b_ref_row)
            k_i = (k * jnp.exp(jnp.minimum(b_ref_row - bc, 0.0))).astype(BF16)
            lhs = jnp.concatenate([q_i * head_masks[h] for h in range(GLA_HEADS)], axis=0).astype(BF16)
            res = lax.dot_general(lhs, k_i, nt, preferred_element_type=F32)
            a_rows.append(jnp.where(col_s < lo_r, res, 0.0))
        o_off = []
        for h in range(GLA_HEADS):
            a_h = jnp.concatenate([a[h * GLA_SUB:(h + 1) * GLA_SUB, :] for a in a_rows], axis=0)
            o_off.append(jnp.dot(a_h.astype(BF16), v[:, h * HEAD_DIM:(h + 1) * HEAD_DIM],
                                 preferred_element_type=F32))
        o = o + jnp.concatenate(o_off, axis=1)

        ksh_ref[GLA_PAD:GLA_PAD + C, :] = k
        bsh_ref[GLA_PAD:GLA_PAD + C, :] = bc
        vsh_ref[GLA_PAD:GLA_PAD + C, :] = vf
        diag = []
        for d in range(GLA_SUB):
            k_s = ksh_ref[GLA_PAD - d:GLA_PAD - d + C, :]
            b_s = bsh_ref[GLA_PAD - d:GLA_PAD - d + C, :]
            diag.append(jnp.where(tmod >= d, q * k_s * jnp.exp(bc - b_s), 0.0).astype(BF16))
        score = jnp.dot(jnp.concatenate(diag, axis=0), e_ref[...], preferred_element_type=F32)
        for d in range(GLA_SUB):
            o = o + score[d * C:(d + 1) * C, :] * vsh_ref[GLA_PAD - d:GLA_PAD - d + C, :]

        g = g_ref[rows, :].astype(F32)
        for h in range(GLA_HEADS):
            hs = slice(h * HEAD_DIM, (h + 1) * HEAD_DIM)
            y = _rms(o[:, hs], gn_ref[:, hs])
            o_ref[rows, hs] = (y * _silu(g[:, hs])).astype(o_ref.dtype)
        return carry

    lax.fori_loop(0, GLA_RB // C, chunk, 0)


def _gla(proj, wup_p, bup3, tril, e_mat, bmask, out_norm3, layer):
    nr = SEQ // GLA_RB
    row = lambda b, r: b * nr + r
    return pl.pallas_call(
        _gla_kernel,
        grid=(BATCH, nr),
        in_specs=[pl.BlockSpec((GLA_RB, GLA_KEY_WIDTH), lambda b, r: (row(b, r), COL_GLA_Q // GLA_KEY_WIDTH)),
                  pl.BlockSpec((GLA_RB, GLA_KEY_WIDTH), lambda b, r: (row(b, r), COL_GLA_K // GLA_KEY_WIDTH)),
                  pl.BlockSpec((GLA_RB, GLA_WIDTH), lambda b, r: (row(b, r), COL_GLA_V // GLA_WIDTH)),
                  pl.BlockSpec((GLA_RB, LANES), lambda b, r: (row(b, r), COL_GLA_R // LANES)),
                  pl.BlockSpec((GLA_RB, GLA_WIDTH), lambda b, r: (row(b, r), COL_GLA_G // GLA_WIDTH)),
                  pl.BlockSpec((None, LANES, GLA_KEY_WIDTH), lambda b, r: (layer, 0, 0)),
                  pl.BlockSpec((None, 1, GLA_KEY_WIDTH), lambda b, r: (layer, 0, 0)),
                  pl.BlockSpec((GLA_CHUNK, GLA_CHUNK), lambda b, r: (0, 0)),
                  pl.BlockSpec((GLA_KEY_WIDTH, GLA_WIDTH), lambda b, r: (0, 0)),
                  pl.BlockSpec((2 * HEAD_DIM, GLA_PAIR), lambda b, r: (0, 0)),
                  pl.BlockSpec((None, 1, GLA_WIDTH), lambda b, r: (layer, 0, SB_WIDTH // GLA_WIDTH))],
        out_specs=pl.BlockSpec((GLA_RB, GLA_WIDTH), lambda b, r: (row(b, r), 0)),
        out_shape=jax.ShapeDtypeStruct((TOKENS, GLA_WIDTH), BF16),
        scratch_shapes=[pltpu.VMEM((GLA_HEADS // 2, 2 * HEAD_DIM, GLA_PAIR), F32),
                        pltpu.VMEM((GLA_PAD + GLA_CHUNK, GLA_KEY_WIDTH), F32),
                        pltpu.VMEM((GLA_PAD + GLA_CHUNK, GLA_KEY_WIDTH), F32),
                        pltpu.VMEM((GLA_PAD + GLA_CHUNK, GLA_WIDTH), F32)],
        compiler_params=_params(("parallel", "arbitrary")),
        name="gla",
    )(proj, proj, proj, proj, proj, wup_p, bup3, tril, e_mat, bmask, out_norm3)


SGU_RB = 512


def _sgu_kernel(u_ref, v_ref, g_ref, w_ref, bb_ref, sn_ref, gn_ref, o_ref):
    C = SGU_CHUNK
    keep = (lax.broadcasted_iota(jnp.int32, (C, C), 0) >= lax.broadcasted_iota(jnp.int32, (C, C), 1))
    w = [jnp.where(keep, w_ref[g], 0.0).astype(BF16) for g in range(SGU_GROUPS)]
    for c in range(SGU_RB // C):
        rows = slice(c * C, (c + 1) * C)
        u = _gelu_tanh(u_ref[rows, :].astype(F32))
        v = _gelu_tanh(v_ref[rows, :].astype(F32))
        vn = _rms(v, sn_ref[...]).astype(BF16)
        gate = g_ref[rows, :].astype(F32)
        for g in range(SGU_GROUPS):
            gs = slice(g * HEAD_DIM, (g + 1) * HEAD_DIM)
            mixed = jnp.dot(w[g], vn[:, gs], preferred_element_type=F32) + bb_ref[g]
            y = _rms(u[:, gs] * mixed, gn_ref[:, gs])
            o_ref[rows, gs] = (y * _silu(gate[:, gs])).astype(o_ref.dtype)


def _sgu(proj, w_sgu, b_sgu_b, sgu_norm3, out_norm3, layer):
    return pl.pallas_call(
        _sgu_kernel,
        grid=(TOKENS // SGU_RB,),
        in_specs=[pl.BlockSpec((SGU_RB, SGU_WIDTH), lambda i: (i, COL_SGU_U // SGU_WIDTH)),
                  pl.BlockSpec((SGU_RB, SGU_WIDTH), lambda i: (i, COL_SGU_V // SGU_WIDTH)),
                  pl.BlockSpec((SGU_RB, SGU_WIDTH), lambda i: (i, COL_SGU_G // SGU_WIDTH)),
                  pl.BlockSpec((None, SGU_GROUPS, SGU_CHUNK, SGU_CHUNK), lambda i: (layer, 0, 0, 0)),
                  pl.BlockSpec((None, SGU_GROUPS, SGU_CHUNK, HEAD_DIM), lambda i: (layer, 0, 0, 0)),
                  pl.BlockSpec((None, 1, SGU_WIDTH), lambda i: (layer, 0, 0)),
                  pl.BlockSpec((None, 1, SGU_WIDTH), lambda i: (layer, 0, (SB_WIDTH + GLA_WIDTH) // SGU_WIDTH))],
        out_specs=pl.BlockSpec((SGU_RB, SGU_WIDTH), lambda i: (i, 0)),
        out_shape=jax.ShapeDtypeStruct((TOKENS, SGU_WIDTH), BF16),
        compiler_params=_params(("parallel",)),
        name="sgu",
    )(proj, proj, proj, w_sgu, b_sgu_b, sgu_norm3, out_norm3)


POST_TM = 256


def _post_kernel(x_ref, msb_ref, mgla_ref, msgu_ref, wout_ref, nx_ref, wxq_ref, kv_ref, wxo_ref, nn_ref,
                 *out_refs, last):
    x1 = x_ref[...]
    x1 = x1 + jnp.dot(msb_ref[...], wout_ref[0:SB_WIDTH, :], preferred_element_type=F32)
    x1 = x1 + jnp.dot(mgla_ref[...], wout_ref[SB_WIDTH:SB_WIDTH + GLA_WIDTH, :], preferred_element_type=F32)
    x1 = x1 + jnp.dot(msgu_ref[...], wout_ref[SB_WIDTH + GLA_WIDTH:, :], preferred_element_type=F32)

    hx = _rms(x1, nx_ref[...]).astype(BF16)
    q = jnp.dot(hx, wxq_ref[...], preferred_element_type=F32).astype(BF16)
    scale = HEAD_DIM ** -0.5
    heads = []
    for h in range(XA_HEADS):
        hs = slice(h * HEAD_DIM, (h + 1) * HEAD_DIM)
        k_h = kv_ref[:, hs]
        v_h = kv_ref[:, XA_WIDTH + h * HEAD_DIM:XA_WIDTH + (h + 1) * HEAD_DIM]
        s = lax.dot_general(q[:, hs], k_h, (((1,), (1,)), ((), ())), preferred_element_type=F32) * scale
        e = jnp.exp(s - jnp.max(s, axis=-1, keepdims=True))
        p = e / jnp.sum(e, axis=-1, keepdims=True)
        heads.append(jnp.dot(p.astype(BF16), v_h, preferred_element_type=F32).astype(BF16))
    x2 = x1 + jnp.dot(jnp.concatenate(heads, axis=1), wxo_ref[...], preferred_element_type=F32)

    if last:
        out_refs[0][...] = _rms(x2, nn_ref[...])
    else:
        out_refs[0][...] = x2
        out_refs[1][...] = _rms(x2, nn_ref[...]).astype(BF16)


def _post(x, m_sb, m_gla, m_sgu, w_out_bf, norm_x3, w_xq_bf, kv, w_xo_bf, next_norm3, layer, next_idx, last):
    tm = POST_TM
    per_b = SEQ // tm
    row_spec = lambda width: pl.BlockSpec((tm, width), lambda i: (i, 0))
    const = lambda *shape: pl.BlockSpec((None,) + shape, lambda i: (layer,) + (0,) * len(shape))
    x_out = jax.ShapeDtypeStruct((TOKENS, D_MODEL), F32)
    if last:
        out_shape, out_specs = x_out, row_spec(D_MODEL)
    else:
        out_shape = (x_out, jax.ShapeDtypeStruct((TOKENS, D_MODEL), BF16))
        out_specs = (row_spec(D_MODEL), row_spec(D_MODEL))
    return pl.pallas_call(
        functools.partial(_post_kernel, last=last),
        grid=(TOKENS // tm,),
        in_specs=[row_spec(D_MODEL), row_spec(SB_WIDTH), row_spec(GLA_WIDTH), row_spec(SGU_WIDTH),
                  const(D_MODEL, D_MODEL),
                  const(1, D_MODEL),
                  const(D_MODEL, XA_WIDTH),
                  pl.BlockSpec((None, None, N_MEM, 2 * XA_WIDTH), lambda i: (layer, i // per_b, 0, 0)),
                  const(XA_WIDTH, D_MODEL),
                  pl.BlockSpec((None, 1, D_MODEL), lambda i: (next_idx, 0, 0))],
        out_specs=out_specs,
        out_shape=out_shape,
        compiler_params=_params(("parallel",)),
        name="post",
    )(x, m_sb, m_gla, m_sgu, w_out_bf, norm_x3, w_xq_bf, kv, w_xo_bf, next_norm3)


def _constants():
    r = jnp.arange(2 * SB_TK)[:, None]
    c = jnp.arange(2 * SB_TK)[None, :]
    tt = jnp.where(c < SB_TK, (r % SB_TK) > c, True).astype(BF16)
    t = jnp.arange(GLA_CHUNK)
    tril = (t[:, None] >= t[None, :]).astype(BF16)
    e_mat = (jnp.arange(GLA_KEY_WIDTH)[:, None] // GLA_HEAD_K
             == jnp.arange(GLA_WIDTH)[None, :] // HEAD_DIM).astype(BF16)
    bmask = (jnp.arange(2 * HEAD_DIM)[:, None] // HEAD_DIM
             == jnp.arange(GLA_PAIR)[None, :] // GLA_HEAD_K).astype(F32)
    return tt, tril, e_mat, bmask


def _layout_w_in(w_in):
    pad = jnp.zeros(w_in.shape[:2] + (LANES - GLA_GATE_RANK,), w_in.dtype)
    return jnp.concatenate([w_in[..., :ORIG_GLA_R], w_in[..., ORIG_GLA_R + GLA_GATE_RANK:],
                            w_in[..., ORIG_GLA_R:ORIG_GLA_R + GLA_GATE_RANK], pad], axis=-1).astype(BF16)


def kernel(x, mem, norm_mix, w_in, w_gla_gate_up, b_gla_gate, sgu_norm, w_sgu, b_sgu, out_norm, w_out,
           norm_xattn, norm_mem, w_xq, w_xkv, w_xo, final_norm):
    tt, tril, e_mat, bmask = _constants()
    w_in_p = _layout_w_in(w_in)
    wup_p = jnp.concatenate(
        [w_gla_gate_up, jnp.zeros((DEPTH, LANES - GLA_GATE_RANK, GLA_KEY_WIDTH), w_gla_gate_up.dtype)],
        axis=1).astype(BF16)
    w_out_bf, w_xq_bf, w_xkv_bf, w_xo_bf = (w.astype(BF16) for w in (w_out, w_xq, w_xkv, w_xo))
    row3 = lambda a: a.reshape(a.shape[0], 1, a.shape[-1])
    norm_mix3, bup3, sgu_norm3, out_norm3 = row3(norm_mix), row3(b_gla_gate), row3(sgu_norm), row3(out_norm)
    norm_x3, norm_mem3 = row3(norm_xattn), row3(norm_mem)
    final3 = final_norm.reshape(1, 1, D_MODEL)
    b_sgu_b = jnp.broadcast_to(b_sgu[..., None], b_sgu.shape + (HEAD_DIM,))

    xf = x.reshape(TOKENS, D_MODEL)
    kv = _mem_kv(mem, norm_mem3, w_xkv_bf)
    h = _norm_rows(xf, norm_mix3, 0)
    for l in range(DEPTH):
        last = l == DEPTH - 1
        proj = _in_proj(h, w_in_p, l)
        m_sb = _sb_attention(proj, tt, out_norm3, l)
        m_gla = _gla(proj, wup_p, bup3, tril, e_mat, bmask, out_norm3, l)
        m_sgu = _sgu(proj, w_sgu, b_sgu_b, sgu_norm3, out_norm3, l)
        nxt = final3 if last else norm_mix3
        res = _post(xf, m_sb, m_gla, m_sgu, w_out_bf, norm_x3, w_xq_bf, kv, w_xo_bf, nxt,
                    l, 0 if last else l + 1, last)
        if last:
            xf = res
        else:
            xf, h = res
    return xf.reshape(BATCH, SEQ, D_MODEL)
```

```python
import functools

import jax
import jax.numpy as jnp
from jax import lax
from jax.experimental import pallas as pl
from jax.experimental.pallas import tpu as pltpu

F32 = jnp.float32
BF16 = jnp.bfloat16

D_MODEL = 2048
BATCH = 4
SEQ = 2048
DEPTH = 4
TOKENS = BATCH * SEQ
HEAD_DIM = 128
SB_HEADS = 8
SB_WIDTH = SB_HEADS * HEAD_DIM
GLA_HEADS = 4
GLA_HEAD_K = 64
GLA_WIDTH = GLA_HEADS * HEAD_DIM
GLA_KEY_WIDTH = GLA_HEADS * GLA_HEAD_K
GLA_GATE_RANK = 16
GLA_GATE_TAU = 16.0
GLA_CHUNK = 64
GLA_SUB = 16
SGU_GROUPS = 4
SGU_WIDTH = 512
SGU_CHUNK = 128
N_MEM = 256
XA_HEADS = 4
XA_WIDTH = XA_HEADS * HEAD_DIM
EPS = 1e-6

LANES = 128

COL_SB_Q = 0
COL_SB_K = SB_WIDTH
COL_SB_V = 2 * SB_WIDTH
COL_SB_G = 3 * SB_WIDTH
COL_GLA_Q = 4 * SB_WIDTH
COL_GLA_K = COL_GLA_Q + GLA_KEY_WIDTH
COL_GLA_V = COL_GLA_K + GLA_KEY_WIDTH
PROJ_A_WIDTH = COL_GLA_V + GLA_WIDTH
COLB_GLA_G = 0
COLB_SGU_U = GLA_WIDTH
COLB_SGU_V = COLB_SGU_U + SGU_WIDTH
COLB_SGU_G = COLB_SGU_V + SGU_WIDTH
COLB_GLA_R = COLB_SGU_G + SGU_WIDTH
PROJ_B_MAIN = COLB_GLA_R
PROJ_B_WIDTH = COLB_GLA_R + LANES
ORIG_GLA_R = PROJ_A_WIDTH

VMEM_LIMIT = 56 * 1024 * 1024


def _params(semantics):
    return pltpu.CompilerParams(dimension_semantics=semantics, vmem_limit_bytes=VMEM_LIMIT)


def _rms(x, g):
    ms = jnp.mean(x * x, axis=-1, keepdims=True)
    return x * lax.rsqrt(ms + EPS) * g


def _silu(g):
    return g / (1.0 + jnp.exp(-g))


def _softplus(z):
    return jnp.maximum(z, 0.0) + jnp.log(1.0 + jnp.exp(-jnp.abs(z)))


def _split_bf16(x):
    hi = x.astype(BF16)
    lo = (x - hi.astype(F32)).astype(BF16)
    return hi, lo


def _gelu_tanh(x):
    return 0.5 * x * (1.0 + jnp.tanh(0.7978845608028654 * (x + 0.044715 * (x * x * x))))


def _norm_kernel(x_ref, g_ref, o_ref):
    o_ref[...] = _rms(x_ref[...], g_ref[...]).astype(o_ref.dtype)


def _norm_rows(x2d, g3d, layer, tm=512):
    rows = x2d.shape[0]
    return pl.pallas_call(
        _norm_kernel,
        grid=(rows // tm,),
        in_specs=[pl.BlockSpec((tm, D_MODEL), lambda i: (i, 0)),
                  pl.BlockSpec((None, 1, D_MODEL), lambda i: (layer, 0, 0))],
        out_specs=pl.BlockSpec((tm, D_MODEL), lambda i: (i, 0)),
        out_shape=jax.ShapeDtypeStruct((rows, D_MODEL), BF16),
        compiler_params=_params(("parallel",)),
        name="rmsnorm_rows",
    )(x2d, g3d)


def _memkv_kernel(m_ref, g_ref, w_ref, o_ref):
    mn = _rms(m_ref[...], g_ref[...]).astype(BF16)
    o_ref[...] = jnp.dot(mn, w_ref[...], preferred_element_type=F32).astype(o_ref.dtype)


def _mem_kv(mem, norm_mem3, w_xkv_bf):
    return pl.pallas_call(
        _memkv_kernel,
        grid=(DEPTH, BATCH),
        in_specs=[pl.BlockSpec((None, N_MEM, D_MODEL), lambda l, b: (b, 0, 0)),
                  pl.BlockSpec((None, 1, D_MODEL), lambda l, b: (l, 0, 0)),
                  pl.BlockSpec((None, D_MODEL, 2 * XA_WIDTH), lambda l, b: (l, 0, 0))],
        out_specs=pl.BlockSpec((None, None, N_MEM, 2 * XA_WIDTH), lambda l, b: (l, b, 0, 0)),
        out_shape=jax.ShapeDtypeStruct((DEPTH, BATCH, N_MEM, 2 * XA_WIDTH), BF16),
        compiler_params=_params(("parallel", "parallel")),
        name="mem_kv",
    )(mem, norm_mem3, w_xkv_bf)


def _proj_a_kernel(h_ref, w_ref, o_ref, wbf_ref):
    @pl.when(pl.program_id(1) == 0)
    def _():
        wbf_ref[...] = w_ref[...].astype(BF16)

    o_ref[...] = jnp.dot(h_ref[...], wbf_ref[...], preferred_element_type=F32).astype(o_ref.dtype)


def _in_proj_a(h, w_in, layer, tm=512, tn=1280):
    return pl.pallas_call(
        _proj_a_kernel,
        grid=(PROJ_A_WIDTH // tn, TOKENS // tm),
        in_specs=[pl.BlockSpec((tm, D_MODEL), lambda n, m: (m, 0)),
                  pl.BlockSpec((None, D_MODEL, tn), lambda n, m: (layer, 0, n))],
        out_specs=pl.BlockSpec((tm, tn), lambda n, m: (m, n)),
        out_shape=jax.ShapeDtypeStruct((TOKENS, PROJ_A_WIDTH), BF16),
        scratch_shapes=[pltpu.VMEM((D_MODEL, tn), BF16)],
        compiler_params=_params(("parallel", "arbitrary")),
        name="in_proj_a",
    )(h, w_in)


def _proj_b_kernel(h_ref, w_ref, wr_ref, o_ref):
    h = h_ref[...]
    o_ref[:, :PROJ_B_MAIN] = jnp.dot(h, w_ref[...], preferred_element_type=F32).astype(o_ref.dtype)
    o_ref[:, PROJ_B_MAIN:] = jnp.dot(h, wr_ref[...], preferred_element_type=F32).astype(o_ref.dtype)


def _in_proj_b(h, w_b, w_r, layer, tm=512):
    return pl.pallas_call(
        _proj_b_kernel,
        grid=(TOKENS // tm,),
        in_specs=[pl.BlockSpec((tm, D_MODEL), lambda m: (m, 0)),
                  pl.BlockSpec((None, D_MODEL, PROJ_B_MAIN), lambda m: (layer, 0, 0)),
                  pl.BlockSpec((None, D_MODEL, LANES), lambda m: (layer, 0, 0))],
        out_specs=pl.BlockSpec((tm, PROJ_B_WIDTH), lambda m: (m, 0)),
        out_shape=jax.ShapeDtypeStruct((TOKENS, PROJ_B_WIDTH), BF16),
        compiler_params=_params(("parallel",)),
        name="in_proj_b",
    )(h, w_b, w_r)


SB_TQ = 256
SB_TK = 128
SB_DONE = -120.0


def _sb_kernel(q_ref, k_ref, v_ref, g_ref, tt_ref, gn_ref, o_ref, acc_ref, carry_ref, z_ref):
    qi = pl.program_id(2)
    q = q_ref[...]
    scale = HEAD_DIM ** -0.5
    nt = (((1,), (1,)), ((), ()))

    def logits(pj):
        k = k_ref[pl.ds(pl.multiple_of(pj * SB_TQ, SB_TQ), SB_TQ), :]
        return lax.dot_general(q, k, nt, preferred_element_type=F32) * scale

    def key_step(pj, diagonal):
        z = logits(pj) if diagonal else z_ref[...]
        z_ref[...] = logits(jnp.maximum(pj - 1, 0))
        v = v_ref[pl.ds(pl.multiple_of(pj * SB_TQ, SB_TQ), SB_TQ), :]
        sp = _softplus(z)
        if diagonal:
            mask = (lax.broadcasted_iota(jnp.int32, (SB_TQ, SB_TQ), 1)
                    < lax.broadcasted_iota(jnp.int32, (SB_TQ, SB_TQ), 0))
            log_keep = jnp.where(mask, -sp, 0.0)
            carry = jnp.zeros((SB_TQ, SB_TK), F32)
        else:
            log_keep = -sp
            carry = carry_ref[...]
        hi, lo = _split_bf16(log_keep)
        tt = tt_ref[...]
        cs_r = jnp.dot(jnp.concatenate([hi[:, SB_TK:], lo[:, SB_TK:]], axis=1), tt,
                       preferred_element_type=F32)
        cs_l = jnp.dot(jnp.concatenate([hi[:, :SB_TK], lo[:, :SB_TK]], axis=1), tt,
                       preferred_element_type=F32)
        log_beta = z - sp
        carry_l = carry + cs_r[:, SB_TK:]
        a = jnp.exp(jnp.concatenate([log_beta[:, :SB_TK] + cs_l[:, :SB_TK] + carry_l,
                                     log_beta[:, SB_TK:] + cs_r[:, :SB_TK] + carry], axis=1))
        if diagonal:
            a = jnp.where(mask, a, 0.0)
        pv = jnp.dot(a.astype(BF16), v, preferred_element_type=F32)
        acc_ref[...] = pv if diagonal else acc_ref[...] + pv
        carry = carry_l + cs_l[:, SB_TK:]
        carry_ref[...] = carry
        return jnp.max(carry)

    def more(state):
        pj, worst = state
        return jnp.logical_and(pj >= 0, worst > SB_DONE)

    def body(state):
        pj, _ = state
        return pj - 1, key_step(pj, False)

    lax.while_loop(more, body, (qi - 1, key_step(qi, True)))
    y = _rms(acc_ref[...], gn_ref[...])
    o_ref[...] = (y * _silu(g_ref[...].astype(F32))).astype(o_ref.dtype)


def _sb_attention(proj, tt, out_norm3, layer):
    nq = SEQ // SB_TQ
    qb, kb, vb, gb = (c // HEAD_DIM for c in (COL_SB_Q, COL_SB_K, COL_SB_V, COL_SB_G))
    return pl.pallas_call(
        _sb_kernel,
        grid=(BATCH, SB_HEADS, nq),
        in_specs=[pl.BlockSpec((SB_TQ, HEAD_DIM), lambda b, h, i: (b * nq + i, qb + h)),
                  pl.BlockSpec((SEQ, HEAD_DIM), lambda b, h, i: (b, kb + h)),
                  pl.BlockSpec((SEQ, HEAD_DIM), lambda b, h, i: (b, vb + h)),
                  pl.BlockSpec((SB_TQ, HEAD_DIM), lambda b, h, i: (b * nq + i, gb + h)),
                  pl.BlockSpec((2 * SB_TK, 2 * SB_TK), lambda b, h, i: (0, 0)),
                  pl.BlockSpec((None, 1, HEAD_DIM), lambda b, h, i: (layer, 0, h))],
        out_specs=pl.BlockSpec((SB_TQ, HEAD_DIM), lambda b, h, i: (b * nq + i, h)),
        out_shape=jax.ShapeDtypeStruct((TOKENS, SB_WIDTH), BF16),
        scratch_shapes=[pltpu.VMEM((SB_TQ, HEAD_DIM), F32), pltpu.VMEM((SB_TQ, SB_TK), F32),
                        pltpu.VMEM((SB_TQ, SB_TQ), F32)],
        compiler_params=_params(("parallel", "parallel", "arbitrary")),
        name="sb_attention",
    )(proj, proj, proj, proj, tt, out_norm3)


GLA_RB = 256
GLA_PAD = GLA_SUB
GLA_PAIR = 2 * GLA_HEAD_K


def _gla_kernel(q_ref, k_ref, v_ref, r_ref, g_ref, wup_ref, bup_ref, tril_ref, e_ref, bm_ref, gn_ref,
                o_ref, st_ref, ksh_ref, bsh_ref, vsh_ref):
    @pl.when(pl.program_id(1) == 0)
    def _():
        st_ref[...] = jnp.zeros_like(st_ref)

    ksh_ref[0:GLA_PAD, :] = jnp.zeros((GLA_PAD, GLA_KEY_WIDTH), F32)
    bsh_ref[0:GLA_PAD, :] = jnp.zeros((GLA_PAD, GLA_KEY_WIDTH), F32)
    vsh_ref[0:GLA_PAD, :] = jnp.zeros((GLA_PAD, GLA_WIDTH), F32)

    C = GLA_CHUNK
    lane_k = lax.broadcasted_iota(jnp.int32, (1, GLA_KEY_WIDTH), 1)
    head_masks = [(lane_k // GLA_HEAD_K == h).astype(F32) for h in range(GLA_HEADS)]
    tmod = lax.broadcasted_iota(jnp.int32, (C, GLA_KEY_WIDTH), 0) % GLA_SUB
    col_s = lax.broadcasted_iota(jnp.int32, (C, C), 1)
    nt = (((1,), (1,)), ((), ()))
    tn = (((0,), (0,)), ((), ()))

    def chunk(c, carry):
        r0 = pl.multiple_of(c * C, C)
        rows = pl.ds(r0, C)
        q = q_ref[rows, :].astype(F32) * (GLA_HEAD_K ** -0.5)
        k = k_ref[rows, :].astype(F32)
        v = v_ref[rows, :]
        vf = v.astype(F32)
        logits = jnp.dot(r_ref[rows, :], wup_ref[...], preferred_element_type=F32) + bup_ref[...]
        log_alpha = (jnp.minimum(logits, 0.0) - jnp.log(1.0 + jnp.exp(-jnp.abs(logits)))) * (1.0 / GLA_GATE_TAU)
        hi, lo = _split_bf16(log_alpha)
        tril = tril_ref[...]
        bc = (jnp.dot(tril, hi, preferred_element_type=F32)
              + jnp.dot(tril, lo, preferred_element_type=F32))
        b_last = bc[C - 1:C, :]

        q_dec = (q * jnp.exp(bc)).astype(BF16)
        k_dec = (k * jnp.exp(b_last - bc)).astype(BF16)
        chunk_decay = jnp.exp(b_last)
        o_parts = []
        for p in range(GLA_HEADS // 2):
            ks = slice(p * GLA_PAIR, (p + 1) * GLA_PAIR)
            vs = slice(p * 2 * HEAD_DIM, (p + 1) * 2 * HEAD_DIM)
            st = st_ref[p]
            o_parts.append(lax.dot_general(q_dec[:, ks], st.astype(BF16), nt, preferred_element_type=F32))
            upd = lax.dot_general(v[:, vs], k_dec[:, ks], tn, preferred_element_type=F32)
            st_ref[p] = st * chunk_decay[:, ks] + upd * bm_ref[...]
        o = jnp.concatenate(o_parts, axis=1)

        a_rows = [jnp.zeros((GLA_HEADS * GLA_SUB, C), F32)]
        for i in range(1, C // GLA_SUB):
            lo_r, hi_r = i * GLA_SUB, (i + 1) * GLA_SUB
            b_ref_row = bc[lo_r - 1:lo_r, :]
            q_i = q[lo_r:hi_r, :] * jnp.exp(bc[lo_r:hi_r, :] - b_ref_row)
            k_i = (k * jnp.exp(jnp.minimum(b_ref_row - bc, 0.0))).astype(BF16)
            lhs = jnp.concatenate([q_i * head_masks[h] for h in range(GLA_HEADS)], axis=0).astype(BF16)
            res = lax.dot_general(lhs, k_i, nt, preferred_element_type=F32)
            a_rows.append(jnp.where(col_s < lo_r, res, 0.0))
        o_off = []
        for h in range(GLA_HEADS):
            a_h = jnp.concatenate([a[h * GLA_SUB:(h + 1) * GLA_SUB, :] for a in a_rows], axis=0)
            o_off.append(jnp.dot(a_h.astype(BF16), v[:, h * HEAD_DIM:(h + 1) * HEAD_DIM],
                                 preferred_element_type=F32))
        o = o + jnp.concatenate(o_off, axis=1)

        ksh_ref[GLA_PAD:GLA_PAD + C, :] = k
        bsh_ref[GLA_PAD:GLA_PAD + C, :] = bc
        vsh_ref[GLA_PAD:GLA_PAD + C, :] = vf
        diag = []
        for d in range(GLA_SUB):
            k_s = ksh_ref[GLA_PAD - d:GLA_PAD - d + C, :]
            b_s = bsh_ref[GLA_PAD - d:GLA_PAD - d + C, :]
            diag.append(jnp.where(tmod >= d, q * k_s * jnp.exp(bc - b_s), 0.0).astype(BF16))
        score = jnp.dot(jnp.concatenate(diag, axis=0), e_ref[...], preferred_element_type=F32)
        for d in range(GLA_SUB):
            o = o + score[d * C:(d + 1) * C, :] * vsh_ref[GLA_PAD - d:GLA_PAD - d + C, :]

        g = g_ref[rows, :].astype(F32)
        for h in range(GLA_HEADS):
            hs = slice(h * HEAD_DIM, (h + 1) * HEAD_DIM)
            y = _rms(o[:, hs], gn_ref[:, hs])
            o_ref[rows, hs] = (y * _silu(g[:, hs])).astype(o_ref.dtype)
        return carry

    lax.fori_loop(0, GLA_RB // C, chunk, 0)


def _gla(proj, proj_b, wup_p, bup3, tril, e_mat, bmask, out_norm3, layer):
    nr = SEQ // GLA_RB
    row = lambda b, r: b * nr + r
    return pl.pallas_call(
        _gla_kernel,
        grid=(BATCH, nr),
        in_specs=[pl.BlockSpec((GLA_RB, GLA_KEY_WIDTH), lambda b, r: (row(b, r), COL_GLA_Q // GLA_KEY_WIDTH)),
                  pl.BlockSpec((GLA_RB, GLA_KEY_WIDTH), lambda b, r: (row(b, r), COL_GLA_K // GLA_KEY_WIDTH)),
                  pl.BlockSpec((GLA_RB, GLA_WIDTH), lambda b, r: (row(b, r), COL_GLA_V // GLA_WIDTH)),
                  pl.BlockSpec((GLA_RB, LANES), lambda b, r: (row(b, r), COLB_GLA_R // LANES)),
                  pl.BlockSpec((GLA_RB, GLA_WIDTH), lambda b, r: (row(b, r), COLB_GLA_G // GLA_WIDTH)),
                  pl.BlockSpec((None, LANES, GLA_KEY_WIDTH), lambda b, r: (layer, 0, 0)),
                  pl.BlockSpec((None, 1, GLA_KEY_WIDTH), lambda b, r: (layer, 0, 0)),
                  pl.BlockSpec((GLA_CHUNK, GLA_CHUNK), lambda b, r: (0, 0)),
                  pl.BlockSpec((GLA_KEY_WIDTH, GLA_WIDTH), lambda b, r: (0, 0)),
                  pl.BlockSpec((2 * HEAD_DIM, GLA_PAIR), lambda b, r: (0, 0)),
                  pl.BlockSpec((None, 1, GLA_WIDTH), lambda b, r: (layer, 0, SB_WIDTH // GLA_WIDTH))],
        out_specs=pl.BlockSpec((GLA_RB, GLA_WIDTH), lambda b, r: (row(b, r), 0)),
        out_shape=jax.ShapeDtypeStruct((TOKENS, GLA_WIDTH), BF16),
        scratch_shapes=[pltpu.VMEM((GLA_HEADS // 2, 2 * HEAD_DIM, GLA_PAIR), F32),
                        pltpu.VMEM((GLA_PAD + GLA_CHUNK, GLA_KEY_WIDTH), F32),
                        pltpu.VMEM((GLA_PAD + GLA_CHUNK, GLA_KEY_WIDTH), F32),
                        pltpu.VMEM((GLA_PAD + GLA_CHUNK, GLA_WIDTH), F32)],
        compiler_params=_params(("parallel", "arbitrary")),
        name="gla",
    )(proj, proj, proj, proj_b, proj_b, wup_p, bup3, tril, e_mat, bmask, out_norm3)


SGU_RB = 512


def _sgu_kernel(u_ref, v_ref, g_ref, w_ref, bb_ref, sn_ref, gn_ref, o_ref):
    C = SGU_CHUNK
    keep = (lax.broadcasted_iota(jnp.int32, (C, C), 0) >= lax.broadcasted_iota(jnp.int32, (C, C), 1))
    w = [jnp.where(keep, w_ref[g], 0.0).astype(BF16) for g in range(SGU_GROUPS)]
    for c in range(SGU_RB // C):
        rows = slice(c * C, (c + 1) * C)
        u = _gelu_tanh(u_ref[rows, :].astype(F32))
        v = _gelu_tanh(v_ref[rows, :].astype(F32))
        vn = _rms(v, sn_ref[...]).astype(BF16)
        gate = g_ref[rows, :].astype(F32)
        for g in range(SGU_GROUPS):
            gs = slice(g * HEAD_DIM, (g + 1) * HEAD_DIM)
            mixed = jnp.dot(w[g], vn[:, gs], preferred_element_type=F32) + bb_ref[g]
            y = _rms(u[:, gs] * mixed, gn_ref[:, gs])
            o_ref[rows, gs] = (y * _silu(gate[:, gs])).astype(o_ref.dtype)


def _sgu(proj, w_sgu, b_sgu_b, sgu_norm3, out_norm3, layer):
    return pl.pallas_call(
        _sgu_kernel,
        grid=(TOKENS // SGU_RB,),
        in_specs=[pl.BlockSpec((SGU_RB, SGU_WIDTH), lambda i: (i, COLB_SGU_U // SGU_WIDTH)),
                  pl.BlockSpec((SGU_RB, SGU_WIDTH), lambda i: (i, COLB_SGU_V // SGU_WIDTH)),
                  pl.BlockSpec((SGU_RB, SGU_WIDTH), lambda i: (i, COLB_SGU_G // SGU_WIDTH)),
                  pl.BlockSpec((None, SGU_GROUPS, SGU_CHUNK, SGU_CHUNK), lambda i: (layer, 0, 0, 0)),
                  pl.BlockSpec((None, SGU_GROUPS, SGU_CHUNK, HEAD_DIM), lambda i: (layer, 0, 0, 0)),
                  pl.BlockSpec((None, 1, SGU_WIDTH), lambda i: (layer, 0, 0)),
                  pl.BlockSpec((None, 1, SGU_WIDTH), lambda i: (layer, 0, (SB_WIDTH + GLA_WIDTH) // SGU_WIDTH))],
        out_specs=pl.BlockSpec((SGU_RB, SGU_WIDTH), lambda i: (i, 0)),
        out_shape=jax.ShapeDtypeStruct((TOKENS, SGU_WIDTH), BF16),
        compiler_params=_params(("parallel",)),
        name="sgu",
    )(proj, proj, proj, w_sgu, b_sgu_b, sgu_norm3, out_norm3)


POST_TM = 256


def _post_kernel(x_ref, msb_ref, mgla_ref, msgu_ref, wout_ref, nx_ref, wxq_ref, kv_ref, wxo_ref, nn_ref,
                 *out_refs, last):
    x1 = x_ref[...]
    x1 = x1 + jnp.dot(msb_ref[...], wout_ref[0:SB_WIDTH, :], preferred_element_type=F32)
    x1 = x1 + jnp.dot(mgla_ref[...], wout_ref[SB_WIDTH:SB_WIDTH + GLA_WIDTH, :], preferred_element_type=F32)
    x1 = x1 + jnp.dot(msgu_ref[...], wout_ref[SB_WIDTH + GLA_WIDTH:, :], preferred_element_type=F32)

    hx = _rms(x1, nx_ref[...]).astype(BF16)
    q = jnp.dot(hx, wxq_ref[...], preferred_element_type=F32).astype(BF16)
    scale = HEAD_DIM ** -0.5
    heads = []
    for h in range(XA_HEADS):
        hs = slice(h * HEAD_DIM, (h + 1) * HEAD_DIM)
        k_h = kv_ref[:, hs]
        v_h = kv_ref[:, XA_WIDTH + h * HEAD_DIM:XA_WIDTH + (h + 1) * HEAD_DIM]
        s = lax.dot_general(q[:, hs], k_h, (((1,), (1,)), ((), ())), preferred_element_type=F32) * scale
        e = jnp.exp(s - jnp.max(s, axis=-1, keepdims=True))
        p = e / jnp.sum(e, axis=-1, keepdims=True)
        heads.append(jnp.dot(p.astype(BF16), v_h, preferred_element_type=F32).astype(BF16))
    x2 = x1 + jnp.dot(jnp.concatenate(heads, axis=1), wxo_ref[...], preferred_element_type=F32)

    if last:
        out_refs[0][...] = _rms(x2, nn_ref[...])
    else:
        out_refs[0][...] = x2
        out_refs[1][...] = _rms(x2, nn_ref[...]).astype(BF16)


def _post(x, m_sb, m_gla, m_sgu, w_out_bf, norm_x3, w_xq_bf, kv, w_xo_bf, next_norm3, layer, next_idx, last):
    tm = POST_TM
    per_b = SEQ // tm
    row_spec = lambda width: pl.BlockSpec((tm, width), lambda i: (i, 0))
    const = lambda *shape: pl.BlockSpec((None,) + shape, lambda i: (layer,) + (0,) * len(shape))
    x_out = jax.ShapeDtypeStruct((TOKENS, D_MODEL), F32)
    if last:
        out_shape, out_specs = x_out, row_spec(D_MODEL)
    else:
        out_shape = (x_out, jax.ShapeDtypeStruct((TOKENS, D_MODEL), BF16))
        out_specs = (row_spec(D_MODEL), row_spec(D_MODEL))
    return pl.pallas_call(
        functools.partial(_post_kernel, last=last),
        grid=(TOKENS // tm,),
        in_specs=[row_spec(D_MODEL), row_spec(SB_WIDTH), row_spec(GLA_WIDTH), row_spec(SGU_WIDTH),
                  const(D_MODEL, D_MODEL),
                  const(1, D_MODEL),
                  const(D_MODEL, XA_WIDTH),
                  pl.BlockSpec((None, None, N_MEM, 2 * XA_WIDTH), lambda i: (layer, i // per_b, 0, 0)),
                  const(XA_WIDTH, D_MODEL),
                  pl.BlockSpec((None, 1, D_MODEL), lambda i: (next_idx, 0, 0))],
        out_specs=out_specs,
        out_shape=out_shape,
        compiler_params=_params(("parallel",)),
        name="post",
    )(x, m_sb, m_gla, m_sgu, w_out_bf, norm_x3, w_xq_bf, kv, w_xo_bf, next_norm3)


def _constants():
    r = jnp.arange(2 * SB_TK)[:, None]
    c = jnp.arange(2 * SB_TK)[None, :]
    tt = jnp.where(c < SB_TK, (r % SB_TK) > c, True).astype(BF16)
    t = jnp.arange(GLA_CHUNK)
    tril = (t[:, None] >= t[None, :]).astype(BF16)
    e_mat = (jnp.arange(GLA_KEY_WIDTH)[:, None] // GLA_HEAD_K
             == jnp.arange(GLA_WIDTH)[None, :] // HEAD_DIM).astype(BF16)
    bmask = (jnp.arange(2 * HEAD_DIM)[:, None] // HEAD_DIM
             == jnp.arange(GLA_PAIR)[None, :] // GLA_HEAD_K).astype(F32)
    return tt, tril, e_mat, bmask


def kernel(x, mem, norm_mix, w_in, w_gla_gate_up, b_gla_gate, sgu_norm, w_sgu, b_sgu, out_norm, w_out,
           norm_xattn, norm_mem, w_xq, w_xkv, w_xo, final_norm):
    tt, tril, e_mat, bmask = _constants()
    w_in_b = w_in[..., ORIG_GLA_R + GLA_GATE_RANK:].astype(BF16)
    w_in_r = jnp.pad(w_in[..., ORIG_GLA_R:ORIG_GLA_R + GLA_GATE_RANK].astype(BF16),
                     ((0, 0), (0, 0), (0, LANES - GLA_GATE_RANK)))
    wup_p = jnp.concatenate(
        [w_gla_gate_up, jnp.zeros((DEPTH, LANES - GLA_GATE_RANK, GLA_KEY_WIDTH), w_gla_gate_up.dtype)],
        axis=1).astype(BF16)
    w_out_bf, w_xq_bf, w_xkv_bf, w_xo_bf = (w.astype(BF16) for w in (w_out, w_xq, w_xkv, w_xo))
    row3 = lambda a: a.reshape(a.shape[0], 1, a.shape[-1])
    norm_mix3, bup3, sgu_norm3, out_norm3 = row3(norm_mix), row3(b_gla_gate), row3(sgu_norm), row3(out_norm)
    norm_x3, norm_mem3 = row3(norm_xattn), row3(norm_mem)
    final3 = final_norm.reshape(1, 1, D_MODEL)
    b_sgu_b = jnp.broadcast_to(b_sgu[..., None], b_sgu.shape + (HEAD_DIM,))

    xf = x.reshape(TOKENS, D_MODEL)
    kv = _mem_kv(mem, norm_mem3, w_xkv_bf)
    h = _norm_rows(xf, norm_mix3, 0)
    for l in range(DEPTH):
        last = l == DEPTH - 1
        proj = _in_proj_a(h, w_in, l)
        proj_b = _in_proj_b(h, w_in_b, w_in_r, l)
        m_sb = _sb_attention(proj, tt, out_norm3, l)
        m_gla = _gla(proj, proj_b, wup_p, bup3, tril, e_mat, bmask, out_norm3, l)
        m_sgu = _sgu(proj_b, w_sgu, b_sgu_b, sgu_norm3, out_norm3, l)
        nxt = final3 if last else norm_mix3
        res = _post(xf, m_sb, m_gla, m_sgu, w_out_bf, norm_x3, w_xq_bf, kv, w_xo_bf, nxt,
                    l, 0 if last else l + 1, last)
        if last:
            xf = res
        else:
            xf, h = res
    return xf.reshape(BATCH, SEQ, D_MODEL)
```

```python
import functools

import jax
import jax.numpy as jnp
from jax import lax
from jax.experimental import pallas as pl
from jax.experimental.pallas import tpu as pltpu

F32 = jnp.float32
BF16 = jnp.bfloat16

D_MODEL = 2048
BATCH = 4
SEQ = 2048
DEPTH = 4
TOKENS = BATCH * SEQ
HEAD_DIM = 128
SB_HEADS = 8
SB_WIDTH = SB_HEADS * HEAD_DIM
GLA_HEADS = 4
GLA_HEAD_K = 64
GLA_WIDTH = GLA_HEADS * HEAD_DIM
GLA_KEY_WIDTH = GLA_HEADS * GLA_HEAD_K
GLA_GATE_RANK = 16
GLA_GATE_TAU = 16.0
GLA_CHUNK = 64
GLA_SUB = 16
SGU_GROUPS = 4
SGU_WIDTH = 512
SGU_CHUNK = 128
N_MEM = 256
XA_HEADS = 4
XA_WIDTH = XA_HEADS * HEAD_DIM
EPS = 1e-6

LANES = 128

COL_SB_Q = 0
COL_SB_K = SB_WIDTH
COL_SB_V = 2 * SB_WIDTH
COL_SB_G = 3 * SB_WIDTH
COL_GLA_Q = 4 * SB_WIDTH
COL_GLA_K = COL_GLA_Q + GLA_KEY_WIDTH
COL_GLA_V = COL_GLA_K + GLA_KEY_WIDTH
PROJ_A_WIDTH = COL_GLA_V + GLA_WIDTH
COLB_GLA_G = 0
COLB_SGU_U = GLA_WIDTH
COLB_SGU_V = COLB_SGU_U + SGU_WIDTH
COLB_SGU_G = COLB_SGU_V + SGU_WIDTH
COLB_GLA_R = COLB_SGU_G + SGU_WIDTH
PROJ_B_MAIN = COLB_GLA_R
PROJ_B_WIDTH = COLB_GLA_R + LANES
ORIG_GLA_R = PROJ_A_WIDTH

VMEM_LIMIT = 56 * 1024 * 1024


def _params(semantics):
    return pltpu.CompilerParams(dimension_semantics=semantics, vmem_limit_bytes=VMEM_LIMIT)


def _rms(x, g):
    ms = jnp.mean(x * x, axis=-1, keepdims=True)
    return x * lax.rsqrt(ms + EPS) * g


def _silu(g):
    return g / (1.0 + jnp.exp(-g))


def _softplus(z):
    return jnp.maximum(z, 0.0) + jnp.log(1.0 + jnp.exp(-jnp.abs(z)))


def _split_bf16(x):
    hi = x.astype(BF16)
    lo = (x - hi.astype(F32)).astype(BF16)
    return hi, lo


def _gelu_tanh(x):
    return 0.5 * x * (1.0 + jnp.tanh(0.7978845608028654 * (x + 0.044715 * (x * x * x))))


def _norm_kernel(x_ref, g_ref, o_ref):
    o_ref[...] = _rms(x_ref[...], g_ref[...]).astype(o_ref.dtype)


def _norm_rows(x2d, g3d, layer, tm=512):
    rows = x2d.shape[0]
    return pl.pallas_call(
        _norm_kernel,
        grid=(rows // tm,),
        in_specs=[pl.BlockSpec((tm, D_MODEL), lambda i: (i, 0)),
                  pl.BlockSpec((None, 1, D_MODEL), lambda i: (layer, 0, 0))],
        out_specs=pl.BlockSpec((tm, D_MODEL), lambda i: (i, 0)),
        out_shape=jax.ShapeDtypeStruct((rows, D_MODEL), BF16),
        compiler_params=_params(("parallel",)),
        name="rmsnorm_rows",
    )(x2d, g3d)


def _memkv_kernel(m_ref, g_ref, w_ref, o_ref):
    mn = _rms(m_ref[...], g_ref[...]).astype(BF16)
    o_ref[...] = jnp.dot(mn, w_ref[...], preferred_element_type=F32).astype(o_ref.dtype)


def _mem_kv(mem, norm_mem3, w_xkv_bf):
    return pl.pallas_call(
        _memkv_kernel,
        grid=(DEPTH, BATCH),
        in_specs=[pl.BlockSpec((None, N_MEM, D_MODEL), lambda l, b: (b, 0, 0)),
                  pl.BlockSpec((None, 1, D_MODEL), lambda l, b: (l, 0, 0)),
                  pl.BlockSpec((None, D_MODEL, 2 * XA_WIDTH), lambda l, b: (l, 0, 0))],
        out_specs=pl.BlockSpec((None, None, N_MEM, 2 * XA_WIDTH), lambda l, b: (l, b, 0, 0)),
        out_shape=jax.ShapeDtypeStruct((DEPTH, BATCH, N_MEM, 2 * XA_WIDTH), BF16),
        compiler_params=_params(("parallel", "parallel")),
        name="mem_kv",
    )(mem, norm_mem3, w_xkv_bf)


W_CONV_ROWS = 256


def _store_transposed_bf16(wt_ref, wbf_ref, n_rows):
    def conv(i, c):
        r0 = pl.multiple_of(i * W_CONV_ROWS, W_CONV_ROWS)
        wbf_ref[:, pl.ds(r0, W_CONV_ROWS)] = wt_ref[pl.ds(r0, W_CONV_ROWS), :].T.astype(BF16)
        return c

    lax.fori_loop(0, n_rows // W_CONV_ROWS, conv, 0)


def _proj_a_kernel(h_ref, wt_ref, o_ref, wbf_ref):
    @pl.when(pl.program_id(1) == 0)
    def _():
        _store_transposed_bf16(wt_ref, wbf_ref, wt_ref.shape[0])

    o_ref[...] = jnp.dot(h_ref[...], wbf_ref[...], preferred_element_type=F32).astype(o_ref.dtype)


def _in_proj_a(h, w_in_t, layer, tm=1024, tn=1280):
    return pl.pallas_call(
        _proj_a_kernel,
        grid=(PROJ_A_WIDTH // tn, TOKENS // tm),
        in_specs=[pl.BlockSpec((tm, D_MODEL), lambda n, m: (m, 0)),
                  pl.BlockSpec((None, tn, D_MODEL), lambda n, m: (layer, n, 0))],
        out_specs=pl.BlockSpec((tm, tn), lambda n, m: (m, n)),
        out_shape=jax.ShapeDtypeStruct((TOKENS, PROJ_A_WIDTH), BF16),
        scratch_shapes=[pltpu.VMEM((D_MODEL, tn), BF16)],
        compiler_params=_params(("parallel", "arbitrary")),
        name="in_proj_a",
    )(h, w_in_t)


def _proj_b_kernel(h_ref, wt_ref, wr_ref, o_ref, wbf_ref):
    @pl.when(pl.program_id(0) == 0)
    def _():
        _store_transposed_bf16(wt_ref, wbf_ref, PROJ_B_MAIN)
        rank = jnp.concatenate([wr_ref[...], jnp.zeros((LANES - GLA_GATE_RANK, D_MODEL), F32)], axis=0)
        wbf_ref[:, PROJ_B_MAIN:] = rank.T.astype(BF16)

    o_ref[...] = jnp.dot(h_ref[...], wbf_ref[...], preferred_element_type=F32).astype(o_ref.dtype)


def _in_proj_b(h, w_in_t, layer, tm=512):
    once = pl.Buffered(1)
    window = lambda rows, start: pl.BlockSpec((None, pl.Element(rows), pl.Element(D_MODEL)),
                                              lambda m: (layer, start, 0), pipeline_mode=once)
    return pl.pallas_call(
        _proj_b_kernel,
        grid=(TOKENS // tm,),
        in_specs=[pl.BlockSpec((tm, D_MODEL), lambda m: (m, 0)),
                  window(PROJ_B_MAIN, ORIG_GLA_R + GLA_GATE_RANK),
                  window(GLA_GATE_RANK, ORIG_GLA_R)],
        out_specs=pl.BlockSpec((tm, PROJ_B_WIDTH), lambda m: (m, 0)),
        out_shape=jax.ShapeDtypeStruct((TOKENS, PROJ_B_WIDTH), BF16),
        scratch_shapes=[pltpu.VMEM((D_MODEL, PROJ_B_WIDTH), BF16)],
        compiler_params=_params(("arbitrary",)),
        name="in_proj_b",
    )(h, w_in_t, w_in_t)


SB_TQ = 256
SB_TK = 128
SB_DONE = -120.0


def _sb_kernel(q_ref, k_ref, v_ref, g_ref, tt_ref, gn_ref, o_ref, acc_ref, carry_ref, z_ref):
    qi = pl.program_id(2)
    q = q_ref[...]
    scale = HEAD_DIM ** -0.5
    nt = (((1,), (1,)), ((), ()))

    def logits(pj):
        k = k_ref[pl.ds(pl.multiple_of(pj * SB_TQ, SB_TQ), SB_TQ), :]
        return lax.dot_general(q, k, nt, preferred_element_type=F32) * scale

    def key_step(pj, diagonal):
        z = logits(pj) if diagonal else z_ref[...]
        z_ref[...] = logits(jnp.maximum(pj - 1, 0))
        v = v_ref[pl.ds(pl.multiple_of(pj * SB_TQ, SB_TQ), SB_TQ), :]
        sp = _softplus(z)
        if diagonal:
            mask = (lax.broadcasted_iota(jnp.int32, (SB_TQ, SB_TQ), 1)
                    < lax.broadcasted_iota(jnp.int32, (SB_TQ, SB_TQ), 0))
            log_keep = jnp.where(mask, -sp, 0.0)
            carry = jnp.zeros((SB_TQ, SB_TK), F32)
        else:
            log_keep = -sp
            carry = carry_ref[...]
        hi, lo = _split_bf16(log_keep)
        tt = tt_ref[...]
        cs_r = jnp.dot(jnp.concatenate([hi[:, SB_TK:], lo[:, SB_TK:]], axis=1), tt,
                       preferred_element_type=F32)
        cs_l = jnp.dot(jnp.concatenate([hi[:, :SB_TK], lo[:, :SB_TK]], axis=1), tt,
                       preferred_element_type=F32)
        log_beta = z - sp
        carry_l = carry + cs_r[:, SB_TK:]
        a = jnp.exp(jnp.concatenate([log_beta[:, :SB_TK] + cs_l[:, :SB_TK] + carry_l,
                                     log_beta[:, SB_TK:] + cs_r[:, :SB_TK] + carry], axis=1))
        if diagonal:
            a = jnp.where(mask, a, 0.0)
        pv = jnp.dot(a.astype(BF16), v, preferred_element_type=F32)
        acc_ref[...] = pv if diagonal else acc_ref[...] + pv
        carry = carry_l + cs_l[:, SB_TK:]
        carry_ref[...] = carry
        return jnp.max(carry)

    def more(state):
        pj, worst = state
        return jnp.logical_and(pj >= 0, worst > SB_DONE)

    def body(state):
        pj, _ = state
        return pj - 1, key_step(pj, False)

    lax.while_loop(more, body, (qi - 1, key_step(qi, True)))
    y = _rms(acc_ref[...], gn_ref[...])
    o_ref[...] = (y * _silu(g_ref[...].astype(F32))).astype(o_ref.dtype)


def _sb_attention(proj, tt, out_norm3, layer):
    nq = SEQ // SB_TQ
    qb, kb, vb, gb = (c // HEAD_DIM for c in (COL_SB_Q, COL_SB_K, COL_SB_V, COL_SB_G))
    return pl.pallas_call(
        _sb_kernel,
        grid=(BATCH, SB_HEADS, nq),
        in_specs=[pl.BlockSpec((SB_TQ, HEAD_DIM), lambda b, h, i: (b * nq + i, qb + h)),
                  pl.BlockSpec((SEQ, HEAD_DIM), lambda b, h, i: (b, kb + h)),
                  pl.BlockSpec((SEQ, HEAD_DIM), lambda b, h, i: (b, vb + h)),
                  pl.BlockSpec((SB_TQ, HEAD_DIM), lambda b, h, i: (b * nq + i, gb + h)),
                  pl.BlockSpec((2 * SB_TK, 2 * SB_TK), lambda b, h, i: (0, 0)),
                  pl.BlockSpec((None, 1, HEAD_DIM), lambda b, h, i: (layer, 0, h))],
        out_specs=pl.BlockSpec((SB_TQ, HEAD_DIM), lambda b, h, i: (b * nq + i, h)),
        out_shape=jax.ShapeDtypeStruct((TOKENS, SB_WIDTH), BF16),
        scratch_shapes=[pltpu.VMEM((SB_TQ, HEAD_DIM), F32), pltpu.VMEM((SB_TQ, SB_TK), F32),
                        pltpu.VMEM((SB_TQ, SB_TQ), F32)],
        compiler_params=_params(("parallel", "parallel", "arbitrary")),
        name="sb_attention",
    )(proj, proj, proj, proj, tt, out_norm3)


GLA_RB = 256
GLA_PAD = GLA_SUB
GLA_PAIR = 2 * GLA_HEAD_K


def _gla_kernel(q_ref, k_ref, v_ref, r_ref, g_ref, wup_ref, bup_ref, tril_ref, e_ref, bm_ref, gn_ref,
                o_ref, st_ref, ksh_ref, bsh_ref):
    @pl.when(pl.program_id(1) == 0)
    def _():
        st_ref[...] = jnp.zeros_like(st_ref)

    for slot in range(2):
        ksh_ref[slot, 0:GLA_PAD, :] = jnp.zeros((GLA_PAD, GLA_KEY_WIDTH), F32)
        bsh_ref[slot, 0:GLA_PAD, :] = jnp.zeros((GLA_PAD, GLA_KEY_WIDTH), F32)

    C = GLA_CHUNK
    lane_k = lax.broadcasted_iota(jnp.int32, (1, GLA_KEY_WIDTH), 1)
    head_masks = [(lane_k // GLA_HEAD_K == h).astype(F32) for h in range(GLA_HEADS)]
    tmod = lax.broadcasted_iota(jnp.int32, (C, GLA_KEY_WIDTH), 0) % GLA_SUB
    col_s = lax.broadcasted_iota(jnp.int32, (C, C), 1)
    lane_head = lax.broadcasted_iota(jnp.int32, (C, LANES), 1) // GLA_SUB
    nt = (((1,), (1,)), ((), ()))
    tn = (((0,), (0,)), ((), ()))

    def chunk(c, slot):
        r0 = pl.multiple_of(c * C, C)
        rows = pl.ds(r0, C)
        q = q_ref[rows, :].astype(F32) * (GLA_HEAD_K ** -0.5)
        k = k_ref[rows, :].astype(F32)
        v = v_ref[rows, :]
        logits = jnp.dot(r_ref[rows, :], wup_ref[...], preferred_element_type=F32) + bup_ref[...]
        log_alpha = (jnp.minimum(logits, 0.0) - jnp.log(1.0 + jnp.exp(-jnp.abs(logits)))) * (1.0 / GLA_GATE_TAU)
        hi, lo = _split_bf16(log_alpha)
        tril = tril_ref[...]
        bc = (jnp.dot(tril, hi, preferred_element_type=F32)
              + jnp.dot(tril, lo, preferred_element_type=F32))
        b_last = bc[C - 1:C, :]

        q_dec = (q * jnp.exp(bc)).astype(BF16)
        k_dec = (k * jnp.exp(b_last - bc)).astype(BF16)
        chunk_decay = jnp.exp(b_last)
        o_parts = []
        for p in range(GLA_HEADS // 2):
            ks = slice(p * GLA_PAIR, (p + 1) * GLA_PAIR)
            vs = slice(p * 2 * HEAD_DIM, (p + 1) * 2 * HEAD_DIM)
            st = st_ref[p]
            o_parts.append(lax.dot_general(q_dec[:, ks], st.astype(BF16), nt, preferred_element_type=F32))
            upd = lax.dot_general(v[:, vs], k_dec[:, ks], tn, preferred_element_type=F32)
            st_ref[p] = st * chunk_decay[:, ks] + upd * bm_ref[...]
        o = jnp.concatenate(o_parts, axis=1)

        a_rows = [jnp.zeros((GLA_HEADS * GLA_SUB, C), F32)]
        for i in range(1, C // GLA_SUB):
            lo_r, hi_r = i * GLA_SUB, (i + 1) * GLA_SUB
            b_ref_row = bc[lo_r - 1:lo_r, :]
            q_i = q[lo_r:hi_r, :] * jnp.exp(bc[lo_r:hi_r, :] - b_ref_row)
            k_i = (k * jnp.exp(jnp.minimum(b_ref_row - bc, 0.0))).astype(BF16)
            lhs = jnp.concatenate([q_i * head_masks[h] for h in range(GLA_HEADS)], axis=0).astype(BF16)
            res = lax.dot_general(lhs, k_i, nt, preferred_element_type=F32)
            a_rows.append(jnp.where(col_s < lo_r, res, 0.0))

        ksh_ref[slot, GLA_PAD:GLA_PAD + C, :] = k
        bsh_ref[slot, GLA_PAD:GLA_PAD + C, :] = bc
        diag = []
        for d in range(GLA_SUB):
            k_s = ksh_ref[slot, GLA_PAD - d:GLA_PAD - d + C, :]
            b_s = bsh_ref[slot, GLA_PAD - d:GLA_PAD - d + C, :]
            diag.append(jnp.where(tmod >= d, q * k_s * jnp.exp(bc - b_s), 0.0).astype(BF16))
        score = jnp.dot(jnp.concatenate(diag, axis=1), e_ref[...], preferred_element_type=F32)

        o_intra = []
        for h in range(GLA_HEADS):
            a_off = jnp.concatenate([a[h * GLA_SUB:(h + 1) * GLA_SUB, :] for a in a_rows], axis=0)
            a_diag = pltpu.roll(jnp.where(lane_head == h, score, 0.0),
                                (LANES - (GLA_SUB - 1) - GLA_SUB * h) % LANES, 1, stride=1, stride_axis=0)
            a_h = a_off + a_diag[:, :C]
            o_intra.append(jnp.dot(a_h.astype(BF16), v[:, h * HEAD_DIM:(h + 1) * HEAD_DIM],
                                   preferred_element_type=F32))
        o = o + jnp.concatenate(o_intra, axis=1)

        g = g_ref[rows, :].astype(F32)
        for h in range(GLA_HEADS):
            hs = slice(h * HEAD_DIM, (h + 1) * HEAD_DIM)
            y = _rms(o[:, hs], gn_ref[:, hs])
            o_ref[rows, hs] = (y * _silu(g[:, hs])).astype(o_ref.dtype)

    def chunk_pair(i, carry):
        chunk(2 * i, 0)
        chunk(2 * i + 1, 1)
        return carry

    lax.fori_loop(0, GLA_RB // (2 * C), chunk_pair, 0)


def _gla(proj, proj_b, wup_p, bup3, tril, e_mat, bmask, out_norm3, layer):
    nr = SEQ // GLA_RB
    row = lambda b, r: b * nr + r
    return pl.pallas_call(
        _gla_kernel,
        grid=(BATCH, nr),
        in_specs=[pl.BlockSpec((GLA_RB, GLA_KEY_WIDTH), lambda b, r: (row(b, r), COL_GLA_Q // GLA_KEY_WIDTH)),
                  pl.BlockSpec((GLA_RB, GLA_KEY_WIDTH), lambda b, r: (row(b, r), COL_GLA_K // GLA_KEY_WIDTH)),
                  pl.BlockSpec((GLA_RB, GLA_WIDTH), lambda b, r: (row(b, r), COL_GLA_V // GLA_WIDTH)),
                  pl.BlockSpec((GLA_RB, LANES), lambda b, r: (row(b, r), COLB_GLA_R // LANES)),
                  pl.BlockSpec((GLA_RB, GLA_WIDTH), lambda b, r: (row(b, r), COLB_GLA_G // GLA_WIDTH)),
                  pl.BlockSpec((None, LANES, GLA_KEY_WIDTH), lambda b, r: (layer, 0, 0)),
                  pl.BlockSpec((None, 1, GLA_KEY_WIDTH), lambda b, r: (layer, 0, 0)),
                  pl.BlockSpec((GLA_CHUNK, GLA_CHUNK), lambda b, r: (0, 0)),
                  pl.BlockSpec((GLA_SUB * GLA_KEY_WIDTH, LANES), lambda b, r: (0, 0)),
                  pl.BlockSpec((2 * HEAD_DIM, GLA_PAIR), lambda b, r: (0, 0)),
                  pl.BlockSpec((None, 1, GLA_WIDTH), lambda b, r: (layer, 0, SB_WIDTH // GLA_WIDTH))],
        out_specs=pl.BlockSpec((GLA_RB, GLA_WIDTH), lambda b, r: (row(b, r), 0)),
        out_shape=jax.ShapeDtypeStruct((TOKENS, GLA_WIDTH), BF16),
        scratch_shapes=[pltpu.VMEM((GLA_HEADS // 2, 2 * HEAD_DIM, GLA_PAIR), F32),
                        pltpu.VMEM((2, GLA_PAD + GLA_CHUNK, GLA_KEY_WIDTH), F32),
                        pltpu.VMEM((2, GLA_PAD + GLA_CHUNK, GLA_KEY_WIDTH), F32)],
        compiler_params=_params(("parallel", "arbitrary")),
        name="gla",
    )(proj, proj, proj, proj_b, proj_b, wup_p, bup3, tril, e_mat, bmask, out_norm3)


SGU_RB = 512


def _sgu_kernel(u_ref, v_ref, g_ref, w_ref, bb_ref, sn_ref, gn_ref, o_ref):
    C = SGU_CHUNK
    keep = (lax.broadcasted_iota(jnp.int32, (C, C), 0) >= lax.broadcasted_iota(jnp.int32, (C, C), 1))
    w = [jnp.where(keep, w_ref[g], 0.0).astype(BF16) for g in range(SGU_GROUPS)]
    for c in range(SGU_RB // C):
        rows = slice(c * C, (c + 1) * C)
        u = _gelu_tanh(u_ref[rows, :].astype(F32))
        v = _gelu_tanh(v_ref[rows, :].astype(F32))
        vn = _rms(v, sn_ref[...]).astype(BF16)
        gate = g_ref[rows, :].astype(F32)
        for g in range(SGU_GROUPS):
            gs = slice(g * HEAD_DIM, (g + 1) * HEAD_DIM)
            mixed = jnp.dot(w[g], vn[:, gs], preferred_element_type=F32) + bb_ref[g]
            y = _rms(u[:, gs] * mixed, gn_ref[:, gs])
            o_ref[rows, gs] = (y * _silu(gate[:, gs])).astype(o_ref.dtype)


def _sgu(proj, w_sgu, b_sgu_b, sgu_norm3, out_norm3, layer):
    return pl.pallas_call(
        _sgu_kernel,
        grid=(TOKENS // SGU_RB,),
        in_specs=[pl.BlockSpec((SGU_RB, SGU_WIDTH), lambda i: (i, COLB_SGU_U // SGU_WIDTH)),
                  pl.BlockSpec((SGU_RB, SGU_WIDTH), lambda i: (i, COLB_SGU_V // SGU_WIDTH)),
                  pl.BlockSpec((SGU_RB, SGU_WIDTH), lambda i: (i, COLB_SGU_G // SGU_WIDTH)),
                  pl.BlockSpec((None, SGU_GROUPS, SGU_CHUNK, SGU_CHUNK), lambda i: (layer, 0, 0, 0)),
                  pl.BlockSpec((None, SGU_GROUPS, SGU_CHUNK, HEAD_DIM), lambda i: (layer, 0, 0, 0)),
                  pl.BlockSpec((None, 1, SGU_WIDTH), lambda i: (layer, 0, 0)),
                  pl.BlockSpec((None, 1, SGU_WIDTH), lambda i: (layer, 0, (SB_WIDTH + GLA_WIDTH) // SGU_WIDTH))],
        out_specs=pl.BlockSpec((SGU_RB, SGU_WIDTH), lambda i: (i, 0)),
        out_shape=jax.ShapeDtypeStruct((TOKENS, SGU_WIDTH), BF16),
        compiler_params=_params(("parallel",)),
        name="sgu",
    )(proj, proj, proj, w_sgu, b_sgu_b, sgu_norm3, out_norm3)


POST_TM = 512


def _post_kernel(x_ref, msb_ref, mgla_ref, msgu_ref, wout_ref, nx_ref, wxq_ref, kv_ref, wxo_ref, nn_ref,
                 *out_refs, last):
    x1 = x_ref[...]
    x1 = x1 + jnp.dot(msb_ref[...], wout_ref[0:SB_WIDTH, :], preferred_element_type=F32)
    x1 = x1 + jnp.dot(mgla_ref[...], wout_ref[SB_WIDTH:SB_WIDTH + GLA_WIDTH, :], preferred_element_type=F32)
    x1 = x1 + jnp.dot(msgu_ref[...], wout_ref[SB_WIDTH + GLA_WIDTH:, :], preferred_element_type=F32)

    hx = _rms(x1, nx_ref[...]).astype(BF16)
    q = jnp.dot(hx, wxq_ref[...], preferred_element_type=F32).astype(BF16)
    scale = HEAD_DIM ** -0.5
    heads = []
    for h in range(XA_HEADS):
        hs = slice(h * HEAD_DIM, (h + 1) * HEAD_DIM)
        k_h = kv_ref[:, hs]
        v_h = kv_ref[:, XA_WIDTH + h * HEAD_DIM:XA_WIDTH + (h + 1) * HEAD_DIM]
        s = lax.dot_general(q[:, hs], k_h, (((1,), (1,)), ((), ())), preferred_element_type=F32) * scale
        e = jnp.exp(s - jnp.max(s, axis=-1, keepdims=True))
        p = e / jnp.sum(e, axis=-1, keepdims=True)
        heads.append(jnp.dot(p.astype(BF16), v_h, preferred_element_type=F32).astype(BF16))
    x2 = x1 + jnp.dot(jnp.concatenate(heads, axis=1), wxo_ref[...], preferred_element_type=F32)

    if last:
        out_refs[0][...] = _rms(x2, nn_ref[...])
    else:
        out_refs[0][...] = x2
        out_refs[1][...] = _rms(x2, nn_ref[...]).astype(BF16)


def _post(x, m_sb, m_gla, m_sgu, w_out_bf, norm_x3, w_xq_bf, kv, w_xo_bf, next_norm3, layer, next_idx, last):
    tm = POST_TM
    per_b = SEQ // tm
    row_spec = lambda width: pl.BlockSpec((tm, width), lambda i: (i, 0))
    const = lambda *shape: pl.BlockSpec((None,) + shape, lambda i: (layer,) + (0,) * len(shape),
                                        pipeline_mode=pl.Buffered(1))
    x_out = jax.ShapeDtypeStruct((TOKENS, D_MODEL), F32)
    if last:
        out_shape, out_specs = x_out, row_spec(D_MODEL)
    else:
        out_shape = (x_out, jax.ShapeDtypeStruct((TOKENS, D_MODEL), BF16))
        out_specs = (row_spec(D_MODEL), row_spec(D_MODEL))
    return pl.pallas_call(
        functools.partial(_post_kernel, last=last),
        grid=(TOKENS // tm,),
        in_specs=[row_spec(D_MODEL), row_spec(SB_WIDTH), row_spec(GLA_WIDTH), row_spec(SGU_WIDTH),
                  const(D_MODEL, D_MODEL),
                  const(1, D_MODEL),
                  const(D_MODEL, XA_WIDTH),
                  pl.BlockSpec((None, None, N_MEM, 2 * XA_WIDTH), lambda i: (layer, i // per_b, 0, 0)),
                  const(XA_WIDTH, D_MODEL),
                  pl.BlockSpec((None, 1, D_MODEL), lambda i: (next_idx, 0, 0))],
        out_specs=out_specs,
        out_shape=out_shape,
        compiler_params=_params(("parallel",)),
        name="post",
    )(x, m_sb, m_gla, m_sgu, w_out_bf, norm_x3, w_xq_bf, kv, w_xo_bf, next_norm3)


def _constants():
    r = jnp.arange(2 * SB_TK)[:, None]
    c = jnp.arange(2 * SB_TK)[None, :]
    tt = jnp.where(c < SB_TK, (r % SB_TK) > c, True).astype(BF16)
    t = jnp.arange(GLA_CHUNK)
    tril = (t[:, None] >= t[None, :]).astype(BF16)
    e_row = jnp.arange(GLA_SUB * GLA_KEY_WIDTH)
    e_lane = GLA_SUB * ((e_row % GLA_KEY_WIDTH) // GLA_HEAD_K) + (GLA_SUB - 1) - e_row // GLA_KEY_WIDTH
    e_mat = (e_lane[:, None] == jnp.arange(LANES)[None, :]).astype(BF16)
    bmask = (jnp.arange(2 * HEAD_DIM)[:, None] // HEAD_DIM
             == jnp.arange(GLA_PAIR)[None, :] // GLA_HEAD_K).astype(F32)
    return tt, tril, e_mat, bmask


def kernel(x, mem, norm_mix, w_in, w_gla_gate_up, b_gla_gate, sgu_norm, w_sgu, b_sgu, out_norm, w_out,
           norm_xattn, norm_mem, w_xq, w_xkv, w_xo, final_norm):
    tt, tril, e_mat, bmask = _constants()
    w_in_t = jnp.swapaxes(w_in, 1, 2)
    wup_p = jnp.pad(w_gla_gate_up, ((0, 0), (0, LANES - GLA_GATE_RANK), (0, 0))).astype(BF16)
    w_out_bf, w_xq_bf, w_xkv_bf, w_xo_bf = (w.astype(BF16) for w in (w_out, w_xq, w_xkv, w_xo))
    row3 = lambda a: a.reshape(a.shape[0], 1, a.shape[-1])
    norm_mix3, bup3, sgu_norm3, out_norm3 = row3(norm_mix), row3(b_gla_gate), row3(sgu_norm), row3(out_norm)
    norm_x3, norm_mem3 = row3(norm_xattn), row3(norm_mem)
    final3 = final_norm.reshape(1, 1, D_MODEL)
    b_sgu_b = jnp.broadcast_to(b_sgu[..., None], b_sgu.shape + (HEAD_DIM,))

    xf = x.reshape(TOKENS, D_MODEL)
    kv = _mem_kv(mem, norm_mem3, w_xkv_bf)
    h = _norm_rows(xf, norm_mix3, 0)
    for l in range(DEPTH):
        last = l == DEPTH - 1
        proj = _in_proj_a(h, w_in_t, l)
        proj_b = _in_proj_b(h, w_in_t, l)
        m_sb = _sb_attention(proj, tt, out_norm3, l)
        m_gla = _gla(proj, proj_b, wup_p, bup3, tril, e_mat, bmask, out_norm3, l)
        m_sgu = _sgu(proj_b, w_sgu, b_sgu_b, sgu_norm3, out_norm3, l)
        nxt = final3 if last else norm_mix3
        res = _post(xf, m_sb, m_gla, m_sgu, w_out_bf, norm_x3, w_xq_bf, kv, w_xo_bf, nxt,
                    l, 0 if last else l + 1, last)
        if last:
            xf = res
        else:
            xf, h = res
    return xf.reshape(BATCH, SEQ, D_MODEL)
```

```python
import functools

import jax
import jax.numpy as jnp
from jax import lax
from jax.experimental import pallas as pl
from jax.experimental.pallas import tpu as pltpu

F32 = jnp.float32
BF16 = jnp.bfloat16

D_MODEL = 2048
BATCH = 4
SEQ = 2048
DEPTH = 4
TOKENS = BATCH * SEQ
HEAD_DIM = 128
SB_HEADS = 8
SB_WIDTH = SB_HEADS * HEAD_DIM
GLA_HEADS = 4
GLA_HEAD_K = 64
GLA_WIDTH = GLA_HEADS * HEAD_DIM
GLA_KEY_WIDTH = GLA_HEADS * GLA_HEAD_K
GLA_GATE_RANK = 16
GLA_GATE_TAU = 16.0
GLA_CHUNK = 64
GLA_SUB = 8
SGU_GROUPS = 4
SGU_WIDTH = 512
SGU_CHUNK = 128
N_MEM = 256
XA_HEADS = 4
XA_WIDTH = XA_HEADS * HEAD_DIM
EPS = 1e-6

LANES = 128

COL_SB_Q = 0
COL_SB_K = SB_WIDTH
COL_SB_V = 2 * SB_WIDTH
COL_SB_G = 3 * SB_WIDTH
COL_GLA_Q = 4 * SB_WIDTH
COL_GLA_K = COL_GLA_Q + GLA_KEY_WIDTH
COL_GLA_V = COL_GLA_K + GLA_KEY_WIDTH
PROJ_A_WIDTH = COL_GLA_V + GLA_WIDTH
COLB_GLA_G = 0
COLB_SGU_U = GLA_WIDTH
COLB_SGU_V = COLB_SGU_U + SGU_WIDTH
COLB_SGU_G = COLB_SGU_V + SGU_WIDTH
COLB_GLA_R = COLB_SGU_G + SGU_WIDTH
PROJ_B_MAIN = COLB_GLA_R
PROJ_B_WIDTH = COLB_GLA_R + LANES
ORIG_GLA_R = PROJ_A_WIDTH

VMEM_LIMIT = 56 * 1024 * 1024


def _params(semantics):
    return pltpu.CompilerParams(dimension_semantics=semantics, vmem_limit_bytes=VMEM_LIMIT)


def _rms(x, g):
    ms = jnp.mean(x * x, axis=-1, keepdims=True)
    return x * lax.rsqrt(ms + EPS) * g


def _silu(g):
    return g / (1.0 + jnp.exp(-g))


def _softplus(z):
    return jnp.maximum(z, 0.0) + jnp.log(1.0 + jnp.exp(-jnp.abs(z)))


def _split_bf16(x):
    hi = x.astype(BF16)
    lo = (x - hi.astype(F32)).astype(BF16)
    return hi, lo


def _gelu_tanh(x):
    return 0.5 * x * (1.0 + jnp.tanh(0.7978845608028654 * (x + 0.044715 * (x * x * x))))


def _norm_kernel(x_ref, g_ref, o_ref):
    o_ref[...] = _rms(x_ref[...], g_ref[...]).astype(o_ref.dtype)


def _norm_rows(x2d, g3d, layer, tm=512):
    rows = x2d.shape[0]
    return pl.pallas_call(
        _norm_kernel,
        grid=(rows // tm,),
        in_specs=[pl.BlockSpec((tm, D_MODEL), lambda i: (i, 0)),
                  pl.BlockSpec((None, 1, D_MODEL), lambda i: (layer, 0, 0))],
        out_specs=pl.BlockSpec((tm, D_MODEL), lambda i: (i, 0)),
        out_shape=jax.ShapeDtypeStruct((rows, D_MODEL), BF16),
        compiler_params=_params(("parallel",)),
        name="rmsnorm_rows",
    )(x2d, g3d)


def _memkv_kernel(m_ref, g_ref, w_ref, o_ref):
    mn = _rms(m_ref[...], g_ref[...]).astype(BF16)
    o_ref[...] = jnp.dot(mn, w_ref[...], preferred_element_type=F32).astype(o_ref.dtype)


def _mem_kv(mem, norm_mem3, w_xkv_bf):
    return pl.pallas_call(
        _memkv_kernel,
        grid=(DEPTH, BATCH),
        in_specs=[pl.BlockSpec((None, N_MEM, D_MODEL), lambda l, b: (b, 0, 0)),
                  pl.BlockSpec((None, 1, D_MODEL), lambda l, b: (l, 0, 0)),
                  pl.BlockSpec((None, D_MODEL, 2 * XA_WIDTH), lambda l, b: (l, 0, 0))],
        out_specs=pl.BlockSpec((None, None, N_MEM, 2 * XA_WIDTH), lambda l, b: (l, b, 0, 0)),
        out_shape=jax.ShapeDtypeStruct((DEPTH, BATCH, N_MEM, 2 * XA_WIDTH), BF16),
        compiler_params=_params(("parallel", "parallel")),
        name="mem_kv",
    )(mem, norm_mem3, w_xkv_bf)


W_CONV_ROWS = 256


def _store_transposed_bf16(wt_ref, wbf_ref, n_rows):
    def conv(i, c):
        r0 = pl.multiple_of(i * W_CONV_ROWS, W_CONV_ROWS)
        wbf_ref[:, pl.ds(r0, W_CONV_ROWS)] = wt_ref[pl.ds(r0, W_CONV_ROWS), :].T.astype(BF16)
        return c

    lax.fori_loop(0, n_rows // W_CONV_ROWS, conv, 0)


def _proj_a_kernel(h_ref, wt_ref, o_ref, wbf_ref):
    @pl.when(pl.program_id(1) == 0)
    def _():
        _store_transposed_bf16(wt_ref, wbf_ref, wt_ref.shape[0])

    o_ref[...] = jnp.dot(h_ref[...], wbf_ref[...], preferred_element_type=F32).astype(o_ref.dtype)


def _in_proj_a(h, w_in_t, layer, tm=1024, tn=1280):
    return pl.pallas_call(
        _proj_a_kernel,
        grid=(PROJ_A_WIDTH // tn, TOKENS // tm),
        in_specs=[pl.BlockSpec((tm, D_MODEL), lambda n, m: (m, 0)),
                  pl.BlockSpec((None, tn, D_MODEL), lambda n, m: (layer, n, 0))],
        out_specs=pl.BlockSpec((tm, tn), lambda n, m: (m, n)),
        out_shape=jax.ShapeDtypeStruct((TOKENS, PROJ_A_WIDTH), BF16),
        scratch_shapes=[pltpu.VMEM((D_MODEL, tn), BF16)],
        compiler_params=_params(("parallel", "arbitrary")),
        name="in_proj_a",
    )(h, w_in_t)


def _proj_b_kernel(h_ref, wt_ref, wr_ref, o_ref, wbf_ref):
    @pl.when(pl.program_id(0) == 0)
    def _():
        _store_transposed_bf16(wt_ref, wbf_ref, PROJ_B_MAIN)
        rank = jnp.concatenate([wr_ref[...], jnp.zeros((LANES - GLA_GATE_RANK, D_MODEL), F32)], axis=0)
        wbf_ref[:, PROJ_B_MAIN:] = rank.T.astype(BF16)

    o_ref[...] = jnp.dot(h_ref[...], wbf_ref[...], preferred_element_type=F32).astype(o_ref.dtype)


def _in_proj_b(h, w_in_t, layer, tm=512):
    once = pl.Buffered(1)
    window = lambda rows, start: pl.BlockSpec((None, pl.Element(rows), pl.Element(D_MODEL)),
                                              lambda m: (layer, start, 0), pipeline_mode=once)
    return pl.pallas_call(
        _proj_b_kernel,
        grid=(TOKENS // tm,),
        in_specs=[pl.BlockSpec((tm, D_MODEL), lambda m: (m, 0)),
                  window(PROJ_B_MAIN, ORIG_GLA_R + GLA_GATE_RANK),
                  window(GLA_GATE_RANK, ORIG_GLA_R)],
        out_specs=pl.BlockSpec((tm, PROJ_B_WIDTH), lambda m: (m, 0)),
        out_shape=jax.ShapeDtypeStruct((TOKENS, PROJ_B_WIDTH), BF16),
        scratch_shapes=[pltpu.VMEM((D_MODEL, PROJ_B_WIDTH), BF16)],
        compiler_params=_params(("arbitrary",)),
        name="in_proj_b",
    )(h, w_in_t, w_in_t)


SB_TQ = 256
SB_TK = 128
SB_NQ = SEQ // SB_TQ
SB_DONE = -120.0
SB_HG = 4
SB_GW = SB_HG * HEAD_DIM


def _sb_head_step(z, v, tt, carry, diagonal):
    sp = _softplus(z)
    if diagonal:
        mask = (lax.broadcasted_iota(jnp.int32, (SB_TQ, SB_TQ), 1)
                < lax.broadcasted_iota(jnp.int32, (SB_TQ, SB_TQ), 0))
        log_keep = jnp.where(mask, -sp, 0.0)
    else:
        log_keep = -sp
    hi, lo = _split_bf16(log_keep)
    cs_r = jnp.dot(jnp.concatenate([hi[:, SB_TK:], lo[:, SB_TK:]], axis=1), tt, preferred_element_type=F32)
    cs_l = jnp.dot(jnp.concatenate([hi[:, :SB_TK], lo[:, :SB_TK]], axis=1), tt, preferred_element_type=F32)
    log_beta = z - sp
    carry_l = carry + cs_r[:, SB_TK:]
    a = jnp.exp(jnp.concatenate([log_beta[:, :SB_TK] + cs_l[:, :SB_TK] + carry_l,
                                 log_beta[:, SB_TK:] + cs_r[:, :SB_TK] + carry], axis=1))
    if diagonal:
        a = jnp.where(mask, a, 0.0)
    return jnp.dot(a.astype(BF16), v, preferred_element_type=F32), carry_l + cs_l[:, SB_TK:]


def _sb_kernel(q_ref, k_ref, v_ref, g_ref, tt_ref, gn_ref, o_ref, acc_ref, carry_ref, z_ref, zd_ref):
    scale = HEAD_DIM ** -0.5
    nt = (((1,), (1,)), ((), ()))
    heads = [slice(i * HEAD_DIM, (i + 1) * HEAD_DIM) for i in range(SB_HG)]

    def tile_rows(t):
        return pl.ds(pl.multiple_of(t * SB_TQ, SB_TQ), SB_TQ)

    def logits(i, qt, kt):
        return lax.dot_general(q_ref[tile_rows(qt), heads[i]], k_ref[tile_rows(kt), heads[i]], nt,
                               preferred_element_type=F32) * scale

    def tile(qi, c):
        def key_step(pj, diagonal):
            worst = None
            for i in range(SB_HG):
                if diagonal:
                    z = zd_ref[i]
                    nxt = jnp.minimum(qi + 1, SB_NQ - 1)
                    zd_ref[i] = logits(i, nxt, nxt)
                    carry = jnp.zeros((SB_TQ, SB_TK), F32)
                else:
                    z = z_ref[i]
                    carry = carry_ref[i]
                z_ref[i] = logits(i, qi, jnp.maximum(pj - 1, 0))
                pv, carry = _sb_head_step(z, v_ref[tile_rows(pj), heads[i]], tt_ref[...], carry, diagonal)
                acc_ref[i] = pv if diagonal else acc_ref[i] + pv
                carry_ref[i] = carry
                worst = jnp.max(carry) if worst is None else jnp.maximum(worst, jnp.max(carry))
            return worst

        def more(state):
            pj, worst = state
            return jnp.logical_and(pj >= 0, worst > SB_DONE)

        def body(state):
            pj, _ = state
            return pj - 1, key_step(pj, False)

        lax.while_loop(more, body, (qi - 1, key_step(qi, True)))
        for i in range(SB_HG):
            y = _rms(acc_ref[i], gn_ref[:, heads[i]])
            gate = g_ref[tile_rows(qi), heads[i]].astype(F32)
            o_ref[tile_rows(qi), heads[i]] = (y * _silu(gate)).astype(o_ref.dtype)
        return c

    for i in range(SB_HG):
        zd_ref[i] = logits(i, 0, 0)
    lax.fori_loop(0, SB_NQ, tile, 0)


def _sb_attention(proj, tt, out_norm3, layer):
    group_cols = lambda first: pl.BlockSpec((SEQ, SB_GW), lambda b, h: (b, first // SB_GW + h))
    return pl.pallas_call(
        _sb_kernel,
        grid=(BATCH, SB_HEADS // SB_HG),
        in_specs=[group_cols(COL_SB_Q), group_cols(COL_SB_K), group_cols(COL_SB_V), group_cols(COL_SB_G),
                  pl.BlockSpec((2 * SB_TK, 2 * SB_TK), lambda b, h: (0, 0)),
                  pl.BlockSpec((None, 1, SB_GW), lambda b, h: (layer, 0, h))],
        out_specs=group_cols(0),
        out_shape=jax.ShapeDtypeStruct((TOKENS, SB_WIDTH), BF16),
        scratch_shapes=[pltpu.VMEM((SB_HG, SB_TQ, HEAD_DIM), F32), pltpu.VMEM((SB_HG, SB_TQ, SB_TK), F32),
                        pltpu.VMEM((SB_HG, SB_TQ, SB_TQ), F32), pltpu.VMEM((SB_HG, SB_TQ, SB_TQ), F32)],
        compiler_params=_params(("parallel", "parallel")),
        name="sb_attention",
    )(proj, proj, proj, proj, tt, out_norm3)


GLA_RB = 256
GLA_NCH = GLA_RB // GLA_CHUNK
GLA_OFF_ROWS = GLA_HEADS * GLA_NCH * GLA_SUB
GLA_PAD = GLA_SUB
GLA_PAIR = 2 * GLA_HEAD_K


def _gla_kernel(q_ref, k_ref, v_ref, r_ref, g_ref, wup_ref, bup_ref, tril_ref, e_ref, bm_ref, om_ref, gn_ref,
                o_ref, st_ref, ksh_ref, bsh_ref):
    @pl.when(pl.program_id(1) == 0)
    def _():
        st_ref[...] = jnp.zeros_like(st_ref)

    R, C, S = GLA_RB, GLA_CHUNK, GLA_SUB
    chunks = [slice(c * C, (c + 1) * C) for c in range(GLA_NCH)]
    lane_k = lax.broadcasted_iota(jnp.int32, (1, GLA_KEY_WIDTH), 1)
    head_masks = [(lane_k // GLA_HEAD_K == h).astype(F32) for h in range(GLA_HEADS)]
    nt = (((1,), (1,)), ((), ()))
    tn = (((0,), (0,)), ((), ()))

    q = q_ref[...].astype(F32) * (GLA_HEAD_K ** -0.5)
    k = k_ref[...].astype(F32)
    v = v_ref[...]
    logits = jnp.dot(r_ref[...], wup_ref[...], preferred_element_type=F32) + bup_ref[...]
    log_alpha = (jnp.minimum(logits, 0.0) - jnp.log(1.0 + jnp.exp(-jnp.abs(logits)))) * (1.0 / GLA_GATE_TAU)
    hi, lo = _split_bf16(log_alpha)
    tril = tril_ref[...]
    bc = (jnp.dot(tril, hi, preferred_element_type=F32)
          + jnp.dot(tril, lo, preferred_element_type=F32))
    b_last = [bc[ch.stop - 1:ch.stop, :] for ch in chunks]

    q_dec = (q * jnp.exp(bc)).astype(BF16)
    k_dec = jnp.concatenate([k[ch] * jnp.exp(b_last[c] - bc[ch]) for c, ch in enumerate(chunks)],
                            axis=0).astype(BF16)
    o_inter = [[None] * (GLA_HEADS // 2) for _ in chunks]
    for p in range(GLA_HEADS // 2):
        ks = slice(p * GLA_PAIR, (p + 1) * GLA_PAIR)
        vs = slice(p * 2 * HEAD_DIM, (p + 1) * 2 * HEAD_DIM)
        st = st_ref[p]
        for c, ch in enumerate(chunks):
            o_inter[c][p] = lax.dot_general(q_dec[ch, ks], st.astype(BF16), nt, preferred_element_type=F32)
            upd = lax.dot_general(v[ch, vs], k_dec[ch, ks], tn, preferred_element_type=F32)
            st = st * jnp.exp(b_last[c][:, ks]) + upd * bm_ref[...]
        st_ref[p] = st
    o = jnp.concatenate([jnp.concatenate(parts, axis=1) for parts in o_inter], axis=0)

    off = []
    for i in range(1, C // S):
        q_parts, k_parts = [], []
        for ch in chunks:
            lo_r = ch.start + i * S
            b_row = bc[lo_r - 1:lo_r, :]
            q_parts.append(q[lo_r:lo_r + S, :] * jnp.exp(bc[lo_r:lo_r + S, :] - b_row))
            k_parts.append(k[ch] * jnp.exp(jnp.minimum(b_row - bc[ch], 0.0)))
        q_i = jnp.concatenate(q_parts, axis=0)
        k_i = jnp.concatenate(k_parts, axis=0).astype(BF16)
        lhs = jnp.concatenate([q_i * head_masks[h] for h in range(GLA_HEADS)], axis=0).astype(BF16)
        off.append(lax.dot_general(lhs, k_i, nt, preferred_element_type=F32) * om_ref[i - 1])

    ksh_ref[0:GLA_PAD, :] = jnp.zeros((GLA_PAD, GLA_KEY_WIDTH), F32)
    bsh_ref[0:GLA_PAD, :] = jnp.zeros((GLA_PAD, GLA_KEY_WIDTH), F32)
    ksh_ref[GLA_PAD:GLA_PAD + R, :] = k
    bsh_ref[GLA_PAD:GLA_PAD + R, :] = bc
    tmod = lax.broadcasted_iota(jnp.int32, (R, GLA_KEY_WIDTH), 0) % S
    diag = []
    for d in range(S):
        k_s = ksh_ref[GLA_PAD - d:GLA_PAD - d + R, :]
        b_s = bsh_ref[GLA_PAD - d:GLA_PAD - d + R, :]
        diag.append(jnp.where(tmod >= d, q * k_s * jnp.exp(bc - b_s), 0.0).astype(BF16))
    score = jnp.dot(jnp.concatenate(diag, axis=1), e_ref[...], preferred_element_type=F32)

    lane_head = lax.broadcasted_iota(jnp.int32, (C, LANES), 1) // S
    no_diag = jnp.zeros((C, LANES), F32)
    o_intra = []
    for h in range(GLA_HEADS):
        blocks = []
        for c, ch in enumerate(chunks):
            a_diag = pltpu.roll(jnp.where(lane_head == h, score[ch], 0.0),
                                (LANES - (S - 1) - S * h + C * c) % LANES, 1, stride=1, stride_axis=0)
            halves = [no_diag] * (R // LANES)
            halves[C * c // LANES] = a_diag
            first = (h * GLA_NCH + c) * S
            a_off = jnp.concatenate([jnp.zeros((S, R), F32)] + [a[first:first + S, :] for a in off], axis=0)
            blocks.append(a_off + jnp.concatenate(halves, axis=1))
        a_h = jnp.concatenate(blocks, axis=0).astype(BF16)
        o_intra.append(jnp.dot(a_h, v[:, h * HEAD_DIM:(h + 1) * HEAD_DIM], preferred_element_type=F32))
    o = o + jnp.concatenate(o_intra, axis=1)

    g = g_ref[...].astype(F32)
    for h in range(GLA_HEADS):
        hs = slice(h * HEAD_DIM, (h + 1) * HEAD_DIM)
        y = _rms(o[:, hs], gn_ref[:, hs])
        o_ref[:, hs] = (y * _silu(g[:, hs])).astype(o_ref.dtype)


def _gla(proj, proj_b, wup_p, bup3, tril, e_mat, bmask, offmask, out_norm3, layer):
    nr = SEQ // GLA_RB
    row = lambda b, r: b * nr + r
    return pl.pallas_call(
        _gla_kernel,
        grid=(BATCH, nr),
        in_specs=[pl.BlockSpec((GLA_RB, GLA_KEY_WIDTH), lambda b, r: (row(b, r), COL_GLA_Q // GLA_KEY_WIDTH)),
                  pl.BlockSpec((GLA_RB, GLA_KEY_WIDTH), lambda b, r: (row(b, r), COL_GLA_K // GLA_KEY_WIDTH)),
                  pl.BlockSpec((GLA_RB, GLA_WIDTH), lambda b, r: (row(b, r), COL_GLA_V // GLA_WIDTH)),
                  pl.BlockSpec((GLA_RB, LANES), lambda b, r: (row(b, r), COLB_GLA_R // LANES)),
                  pl.BlockSpec((GLA_RB, GLA_WIDTH), lambda b, r: (row(b, r), COLB_GLA_G // GLA_WIDTH)),
                  pl.BlockSpec((None, LANES, GLA_KEY_WIDTH), lambda b, r: (layer, 0, 0)),
                  pl.BlockSpec((None, 1, GLA_KEY_WIDTH), lambda b, r: (layer, 0, 0)),
                  pl.BlockSpec((GLA_RB, GLA_RB), lambda b, r: (0, 0)),
                  pl.BlockSpec((GLA_SUB * GLA_KEY_WIDTH, LANES), lambda b, r: (0, 0)),
                  pl.BlockSpec((2 * HEAD_DIM, GLA_PAIR), lambda b, r: (0, 0)),
                  pl.BlockSpec((GLA_CHUNK // GLA_SUB - 1, GLA_OFF_ROWS, GLA_RB), lambda b, r: (0, 0, 0)),
                  pl.BlockSpec((None, 1, GLA_WIDTH), lambda b, r: (layer, 0, SB_WIDTH // GLA_WIDTH))],
        out_specs=pl.BlockSpec((GLA_RB, GLA_WIDTH), lambda b, r: (row(b, r), 0)),
        out_shape=jax.ShapeDtypeStruct((TOKENS, GLA_WIDTH), BF16),
        scratch_shapes=[pltpu.VMEM((GLA_HEADS // 2, 2 * HEAD_DIM, GLA_PAIR), F32),
                        pltpu.VMEM((GLA_PAD + GLA_RB, GLA_KEY_WIDTH), F32),
                        pltpu.VMEM((GLA_PAD + GLA_RB, GLA_KEY_WIDTH), F32)],
        compiler_params=_params(("parallel", "arbitrary")),
        name="gla",
    )(proj, proj, proj, proj_b, proj_b, wup_p, bup3, tril, e_mat, bmask, offmask, out_norm3)


SGU_RB = 512


def _sgu_kernel(u_ref, v_ref, g_ref, w_ref, bb_ref, sn_ref, gn_ref, o_ref):
    C = SGU_CHUNK
    keep = (lax.broadcasted_iota(jnp.int32, (C, C), 0) >= lax.broadcasted_iota(jnp.int32, (C, C), 1))
    w = [jnp.where(keep, w_ref[g], 0.0).astype(BF16) for g in range(SGU_GROUPS)]
    for c in range(SGU_RB // C):
        rows = slice(c * C, (c + 1) * C)
        u = _gelu_tanh(u_ref[rows, :].astype(F32))
        v = _gelu_tanh(v_ref[rows, :].astype(F32))
        vn = _rms(v, sn_ref[...]).astype(BF16)
        gate = g_ref[rows, :].astype(F32)
        for g in range(SGU_GROUPS):
            gs = slice(g * HEAD_DIM, (g + 1) * HEAD_DIM)
            mixed = jnp.dot(w[g], vn[:, gs], preferred_element_type=F32) + bb_ref[g]
            y = _rms(u[:, gs] * mixed, gn_ref[:, gs])
            o_ref[rows, gs] = (y * _silu(gate[:, gs])).astype(o_ref.dtype)


def _sgu(proj, w_sgu, b_sgu_b, sgu_norm3, out_norm3, layer):
    return pl.pallas_call(
        _sgu_kernel,
        grid=(TOKENS // SGU_RB,),
        in_specs=[pl.BlockSpec((SGU_RB, SGU_WIDTH), lambda i: (i, COLB_SGU_U // SGU_WIDTH)),
                  pl.BlockSpec((SGU_RB, SGU_WIDTH), lambda i: (i, COLB_SGU_V // SGU_WIDTH)),
                  pl.BlockSpec((SGU_RB, SGU_WIDTH), lambda i: (i, COLB_SGU_G // SGU_WIDTH)),
                  pl.BlockSpec((None, SGU_GROUPS, SGU_CHUNK, SGU_CHUNK), lambda i: (layer, 0, 0, 0)),
                  pl.BlockSpec((None, SGU_GROUPS, SGU_CHUNK, HEAD_DIM), lambda i: (layer, 0, 0, 0)),
                  pl.BlockSpec((None, 1, SGU_WIDTH), lambda i: (layer, 0, 0)),
                  pl.BlockSpec((None, 1, SGU_WIDTH), lambda i: (layer, 0, (SB_WIDTH + GLA_WIDTH) // SGU_WIDTH))],
        out_specs=pl.BlockSpec((SGU_RB, SGU_WIDTH), lambda i: (i, 0)),
        out_shape=jax.ShapeDtypeStruct((TOKENS, SGU_WIDTH), BF16),
        compiler_params=_params(("parallel",)),
        name="sgu",
    )(proj, proj, proj, w_sgu, b_sgu_b, sgu_norm3, out_norm3)


POST_TM = 512


def _post_kernel(x_ref, msb_ref, mgla_ref, msgu_ref, wout_ref, nx_ref, wxq_ref, kv_ref, wxo_ref, nn_ref,
                 *out_refs, last):
    mix = jnp.concatenate([msb_ref[...], mgla_ref[...], msgu_ref[...]], axis=1)
    x1 = x_ref[...] + jnp.dot(mix, wout_ref[...], preferred_element_type=F32)

    hx = _rms(x1, nx_ref[...]).astype(BF16)
    q = jnp.dot(hx, wxq_ref[...], preferred_element_type=F32).astype(BF16)
    scale = HEAD_DIM ** -0.5
    heads = []
    for h in range(XA_HEADS):
        hs = slice(h * HEAD_DIM, (h + 1) * HEAD_DIM)
        k_h = kv_ref[:, hs]
        v_h = kv_ref[:, XA_WIDTH + h * HEAD_DIM:XA_WIDTH + (h + 1) * HEAD_DIM]
        s = lax.dot_general(q[:, hs], k_h, (((1,), (1,)), ((), ())), preferred_element_type=F32) * scale
        e = jnp.exp(s - jnp.max(s, axis=-1, keepdims=True))
        p = e / jnp.sum(e, axis=-1, keepdims=True)
        heads.append(jnp.dot(p.astype(BF16), v_h, preferred_element_type=F32).astype(BF16))
    x2 = x1 + jnp.dot(jnp.concatenate(heads, axis=1), wxo_ref[...], preferred_element_type=F32)

    if last:
        out_refs[0][...] = _rms(x2, nn_ref[...])
    else:
        out_refs[0][...] = x2
        out_refs[1][...] = _rms(x2, nn_ref[...]).astype(BF16)


def _post(x, m_sb, m_gla, m_sgu, w_out_bf, norm_x3, w_xq_bf, kv, w_xo_bf, next_norm3, layer, next_idx, last):
    tm = POST_TM
    per_b = SEQ // tm
    row_spec = lambda width: pl.BlockSpec((tm, width), lambda i: (i, 0))
    const = lambda *shape: pl.BlockSpec((None,) + shape, lambda i: (layer,) + (0,) * len(shape),
                                        pipeline_mode=pl.Buffered(1))
    x_out = jax.ShapeDtypeStruct((TOKENS, D_MODEL), F32)
    if last:
        out_shape, out_specs = x_out, row_spec(D_MODEL)
    else:
        out_shape = (x_out, jax.ShapeDtypeStruct((TOKENS, D_MODEL), BF16))
        out_specs = (row_spec(D_MODEL), row_spec(D_MODEL))
    return pl.pallas_call(
        functools.partial(_post_kernel, last=last),
        grid=(TOKENS // tm,),
        in_specs=[row_spec(D_MODEL), row_spec(SB_WIDTH), row_spec(GLA_WIDTH), row_spec(SGU_WIDTH),
                  const(D_MODEL, D_MODEL),
                  const(1, D_MODEL),
                  const(D_MODEL, XA_WIDTH),
                  pl.BlockSpec((None, None, N_MEM, 2 * XA_WIDTH), lambda i: (layer, i // per_b, 0, 0)),
                  const(XA_WIDTH, D_MODEL),
                  pl.BlockSpec((None, 1, D_MODEL), lambda i: (next_idx, 0, 0))],
        out_specs=out_specs,
        out_shape=out_shape,
        compiler_params=_params(("parallel",)),
        name="post",
    )(x, m_sb, m_gla, m_sgu, w_out_bf, norm_x3, w_xq_bf, kv, w_xo_bf, next_norm3)


def _constants():
    r = jnp.arange(2 * SB_TK)[:, None]
    c = jnp.arange(2 * SB_TK)[None, :]
    tt = jnp.where(c < SB_TK, (r % SB_TK) > c, True).astype(BF16)
    t = jnp.arange(GLA_RB)
    tril = ((t[:, None] >= t[None, :]) & (t[:, None] // GLA_CHUNK == t[None, :] // GLA_CHUNK)).astype(BF16)
    row_chunk = (jnp.arange(GLA_OFF_ROWS)[:, None] // GLA_SUB) % GLA_NCH
    offmask = jnp.stack([(row_chunk == t[None, :] // GLA_CHUNK) & (t[None, :] % GLA_CHUNK < i * GLA_SUB)
                         for i in range(1, GLA_CHUNK // GLA_SUB)]).astype(F32)
    e_row = jnp.arange(GLA_SUB * GLA_KEY_WIDTH)
    e_lane = GLA_SUB * ((e_row % GLA_KEY_WIDTH) // GLA_HEAD_K) + (GLA_SUB - 1) - e_row // GLA_KEY_WIDTH
    e_mat = (e_lane[:, None] == jnp.arange(LANES)[None, :]).astype(BF16)
    bmask = (jnp.arange(2 * HEAD_DIM)[:, None] // HEAD_DIM
             == jnp.arange(GLA_PAIR)[None, :] // GLA_HEAD_K).astype(F32)
    return tt, tril, e_mat, bmask, offmask


def kernel(x, mem, norm_mix, w_in, w_gla_gate_up, b_gla_gate, sgu_norm, w_sgu, b_sgu, out_norm, w_out,
           norm_xattn, norm_mem, w_xq, w_xkv, w_xo, final_norm):
    tt, tril, e_mat, bmask, offmask = _constants()
    w_in_t = jnp.swapaxes(w_in, 1, 2)
    wup_p = jnp.pad(w_gla_gate_up, ((0, 0), (0, LANES - GLA_GATE_RANK), (0, 0))).astype(BF16)
    w_out_bf, w_xq_bf, w_xkv_bf, w_xo_bf = (w.astype(BF16) for w in (w_out, w_xq, w_xkv, w_xo))
    row3 = lambda a: a.reshape(a.shape[0], 1, a.shape[-1])
    norm_mix3, bup3, sgu_norm3, out_norm3 = row3(norm_mix), row3(b_gla_gate), row3(sgu_norm), row3(out_norm)
    norm_x3, norm_mem3 = row3(norm_xattn), row3(norm_mem)
    final3 = final_norm.reshape(1, 1, D_MODEL)
    b_sgu_b = jnp.broadcast_to(b_sgu[..., None], b_sgu.shape + (HEAD_DIM,))

    xf = x.reshape(TOKENS, D_MODEL)
    kv = _mem_kv(mem, norm_mem3, w_xkv_bf)
    h = _norm_rows(xf, norm_mix3, 0)
    for l in range(DEPTH):
        last = l == DEPTH - 1
        proj = _in_proj_a(h, w_in_t, l)
        proj_b = _in_proj_b(h, w_in_t, l)
        m_sb = _sb_attention(proj, tt, out_norm3, l)
        m_gla = _gla(proj, proj_b, wup_p, bup3, tril, e_mat, bmask, offmask, out_norm3, l)
        m_sgu = _sgu(proj_b, w_sgu, b_sgu_b, sgu_norm3, out_norm3, l)
        nxt = final3 if last else norm_mix3
        res = _post(xf, m_sb, m_gla, m_sgu, w_out_bf, norm_x3, w_xq_bf, kv, w_xo_bf, nxt,
                    l, 0 if last else l + 1, last)
        if last:
            xf = res
        else:
            xf, h = res
    return xf.reshape(BATCH, SEQ, D_MODEL)
```

```python
import functools

import jax
import jax.numpy as jnp
from jax import lax
from jax.experimental import pallas as pl
from jax.experimental.pallas import tpu as pltpu

F32 = jnp.float32
BF16 = jnp.bfloat16

D_MODEL = 2048
BATCH = 4
SEQ = 2048
DEPTH = 4
TOKENS = BATCH * SEQ
HEAD_DIM = 128
SB_HEADS = 8
SB_WIDTH = SB_HEADS * HEAD_DIM
GLA_HEADS = 4
GLA_HEAD_K = 64
GLA_WIDTH = GLA_HEADS * HEAD_DIM
GLA_KEY_WIDTH = GLA_HEADS * GLA_HEAD_K
GLA_GATE_RANK = 16
GLA_GATE_TAU = 16.0
GLA_CHUNK = 64
GLA_SUB = 8
SGU_GROUPS = 4
SGU_WIDTH = 512
SGU_CHUNK = 128
N_MEM = 256
XA_HEADS = 4
XA_WIDTH = XA_HEADS * HEAD_DIM
EPS = 1e-6

LANES = 128

COL_SB_Q = 0
COL_SB_K = SB_WIDTH
COL_SB_V = 2 * SB_WIDTH
COL_SB_G = 3 * SB_WIDTH
COL_GLA_Q = 4 * SB_WIDTH
COL_GLA_K = COL_GLA_Q + GLA_KEY_WIDTH
COL_GLA_V = COL_GLA_K + GLA_KEY_WIDTH
PROJ_A_WIDTH = COL_GLA_V + GLA_WIDTH
COLB_GLA_G = 0
COLB_SGU_U = GLA_WIDTH
COLB_SGU_V = COLB_SGU_U + SGU_WIDTH
COLB_SGU_G = COLB_SGU_V + SGU_WIDTH
COLB_GLA_R = COLB_SGU_G + SGU_WIDTH
PROJ_B_MAIN = COLB_GLA_R
PROJ_B_WIDTH = COLB_GLA_R + LANES
ORIG_GLA_R = PROJ_A_WIDTH

VMEM_LIMIT = 56 * 1024 * 1024


def _params(semantics):
    return pltpu.CompilerParams(dimension_semantics=semantics, vmem_limit_bytes=VMEM_LIMIT)


def _rms(x, g):
    ms = jnp.mean(x * x, axis=-1, keepdims=True)
    return x * lax.rsqrt(ms + EPS) * g


def _silu(g):
    return g / (1.0 + jnp.exp(-g))


def _softplus(z):
    return jnp.maximum(z, 0.0) + jnp.log(1.0 + jnp.exp(-jnp.abs(z)))


def _split_bf16(x):
    hi = x.astype(BF16)
    lo = (x - hi.astype(F32)).astype(BF16)
    return hi, lo


def _gelu_tanh(x):
    return 0.5 * x * (1.0 + jnp.tanh(0.7978845608028654 * (x + 0.044715 * (x * x * x))))


def _norm_kernel(x_ref, g_ref, o_ref):
    o_ref[...] = _rms(x_ref[...], g_ref[...]).astype(o_ref.dtype)


def _norm_rows(x2d, g3d, layer, tm=512):
    rows = x2d.shape[0]
    return pl.pallas_call(
        _norm_kernel,
        grid=(rows // tm,),
        in_specs=[pl.BlockSpec((tm, D_MODEL), lambda i: (i, 0)),
                  pl.BlockSpec((None, 1, D_MODEL), lambda i: (layer, 0, 0))],
        out_specs=pl.BlockSpec((tm, D_MODEL), lambda i: (i, 0)),
        out_shape=jax.ShapeDtypeStruct((rows, D_MODEL), BF16),
        compiler_params=_params(("parallel",)),
        name="rmsnorm_rows",
    )(x2d, g3d)


def _memkv_kernel(m_ref, g_ref, w_ref, o_ref):
    mn = _rms(m_ref[...], g_ref[...]).astype(BF16)
    o_ref[...] = jnp.dot(mn, w_ref[...].astype(BF16), preferred_element_type=F32).astype(o_ref.dtype)


def _mem_kv(mem2d, norm_mem3, w_xkv):
    rows = mem2d.shape[0]
    return pl.pallas_call(
        _memkv_kernel,
        grid=(DEPTH,),
        in_specs=[pl.BlockSpec((rows, D_MODEL), lambda l: (0, 0)),
                  pl.BlockSpec((None, 1, D_MODEL), lambda l: (l, 0, 0)),
                  pl.BlockSpec((None, D_MODEL, 2 * XA_WIDTH), lambda l: (l, 0, 0))],
        out_specs=pl.BlockSpec((None, rows, 2 * XA_WIDTH), lambda l: (l, 0, 0)),
        out_shape=jax.ShapeDtypeStruct((DEPTH, rows, 2 * XA_WIDTH), BF16),
        compiler_params=_params(("parallel",)),
        name="mem_kv",
    )(mem2d, norm_mem3, w_xkv)


W_CONV_ROWS = 256


def _store_transposed_bf16(wt_ref, wbf_ref, n_rows):
    def conv(i, c):
        r0 = pl.multiple_of(i * W_CONV_ROWS, W_CONV_ROWS)
        wbf_ref[:, pl.ds(r0, W_CONV_ROWS)] = wt_ref[pl.ds(r0, W_CONV_ROWS), :].T.astype(BF16)
        return c

    lax.fori_loop(0, n_rows // W_CONV_ROWS, conv, 0)


def _proj_a_kernel(h_ref, wt_ref, o_ref, wbf_ref):
    @pl.when(pl.program_id(1) == 0)
    def _():
        _store_transposed_bf16(wt_ref, wbf_ref, wt_ref.shape[0])

    o_ref[...] = jnp.dot(h_ref[...], wbf_ref[...], preferred_element_type=F32).astype(o_ref.dtype)


def _in_proj_a(h, w_in_t, layer, tm=1024, tn=1280):
    return pl.pallas_call(
        _proj_a_kernel,
        grid=(PROJ_A_WIDTH // tn, TOKENS // tm),
        in_specs=[pl.BlockSpec((tm, D_MODEL), lambda n, m: (m, 0)),
                  pl.BlockSpec((None, tn, D_MODEL), lambda n, m: (layer, n, 0))],
        out_specs=pl.BlockSpec((tm, tn), lambda n, m: (m, n)),
        out_shape=jax.ShapeDtypeStruct((TOKENS, PROJ_A_WIDTH), BF16),
        scratch_shapes=[pltpu.VMEM((D_MODEL, tn), BF16)],
        compiler_params=_params(("parallel", "arbitrary")),
        name="in_proj_a",
    )(h, w_in_t)


def _proj_b_kernel(h_ref, wt_ref, wr_ref, o_ref, wbf_ref):
    @pl.when(pl.program_id(0) == 0)
    def _():
        _store_transposed_bf16(wt_ref, wbf_ref, PROJ_B_MAIN)
        rank = jnp.concatenate([wr_ref[...], jnp.zeros((LANES - GLA_GATE_RANK, D_MODEL), F32)], axis=0)
        wbf_ref[:, PROJ_B_MAIN:] = rank.T.astype(BF16)

    o_ref[...] = jnp.dot(h_ref[...], wbf_ref[...], preferred_element_type=F32).astype(o_ref.dtype)


def _in_proj_b(h, w_in_t, layer, tm=1024):
    once = pl.Buffered(1)
    window = lambda rows, start: pl.BlockSpec((None, pl.Element(rows), pl.Element(D_MODEL)),
                                              lambda m: (layer, start, 0), pipeline_mode=once)
    return pl.pallas_call(
        _proj_b_kernel,
        grid=(TOKENS // tm,),
        in_specs=[pl.BlockSpec((tm, D_MODEL), lambda m: (m, 0)),
                  window(PROJ_B_MAIN, ORIG_GLA_R + GLA_GATE_RANK),
                  window(GLA_GATE_RANK, ORIG_GLA_R)],
        out_specs=pl.BlockSpec((tm, PROJ_B_WIDTH), lambda m: (m, 0)),
        out_shape=jax.ShapeDtypeStruct((TOKENS, PROJ_B_WIDTH), BF16),
        scratch_shapes=[pltpu.VMEM((D_MODEL, PROJ_B_WIDTH), BF16)],
        compiler_params=_params(("arbitrary",)),
        name="in_proj_b",
    )(h, w_in_t, w_in_t)


SB_TQ = 256
SB_TK = 128
SB_NQ = SEQ // SB_TQ
SB_DONE = -120.0
SB_HG = 4
SB_GW = SB_HG * HEAD_DIM


def _sb_suffix(log_keep, tt):
    hi, lo = _split_bf16(log_keep)
    return jnp.dot(jnp.concatenate([hi, lo], axis=1), tt, preferred_element_type=F32)


def _sb_step(z, v, tt, carry, causal):
    sp = _softplus(z)
    log_keep = -sp if causal is None else jnp.where(causal, -sp, 0.0)
    cs_r = _sb_suffix(log_keep[:, SB_TK:], tt)
    cs_l = _sb_suffix(log_keep[:, :SB_TK], tt)
    log_beta = z - sp
    arg_r = log_beta[:, SB_TK:] + cs_r[:, :SB_TK]
    carry_l = cs_r[:, SB_TK:]
    if carry is not None:
        arg_r = arg_r + carry
        carry_l = carry_l + carry
    a = jnp.exp(jnp.concatenate([log_beta[:, :SB_TK] + cs_l[:, :SB_TK] + carry_l, arg_r], axis=1))
    if causal is not None:
        a = jnp.where(causal, a, 0.0)
    return jnp.dot(a.astype(BF16), v, preferred_element_type=F32), carry_l + cs_l[:, SB_TK:]


def _sb_kernel(q_ref, k_ref, v_ref, g_ref, tt_ref, gn_ref, o_ref, acc_ref, carry_ref, z_ref, zd_ref, z1_ref):
    scale = HEAD_DIM ** -0.5
    nt = (((1,), (1,)), ((), ()))
    heads = [slice(i * HEAD_DIM, (i + 1) * HEAD_DIM) for i in range(SB_HG)]
    causal = (lax.broadcasted_iota(jnp.int32, (SB_TQ, SB_TQ), 1)
              < lax.broadcasted_iota(jnp.int32, (SB_TQ, SB_TQ), 0))

    def tile_rows(t):
        if isinstance(t, int):
            return slice(t * SB_TQ, (t + 1) * SB_TQ)
        return pl.ds(pl.multiple_of(t * SB_TQ, SB_TQ), SB_TQ)

    def logits(i, qt, kt):
        return lax.dot_general(q_ref[tile_rows(qt), heads[i]], k_ref[tile_rows(kt), heads[i]], nt,
                               preferred_element_type=F32) * scale

    def finish(qi):
        for i in range(SB_HG):
            y = _rms(acc_ref[i], gn_ref[:, heads[i]])
            gate = g_ref[tile_rows(qi), heads[i]].astype(F32)
            o_ref[tile_rows(qi), heads[i]] = (y * _silu(gate)).astype(o_ref.dtype)

    def prefetch_next_tile(i, qi):
        nxt = jnp.minimum(qi + 1, SB_NQ - 1)
        zd_ref[i] = logits(i, nxt, nxt)
        z1_ref[i] = logits(i, nxt, nxt - 1)

    def tile(qi, c):
        worst = None
        for i in range(SB_HG):
            zd, z1 = zd_ref[i], z1_ref[i]
            prefetch_next_tile(i, qi)
            z_ref[i] = logits(i, qi, jnp.maximum(qi - 2, 0))
            pv0, carry = _sb_step(zd, v_ref[tile_rows(qi), heads[i]], tt_ref[...], None, causal)
            pv1, carry = _sb_step(z1, v_ref[tile_rows(qi - 1), heads[i]], tt_ref[...], carry, None)
            acc_ref[i] = pv0 + pv1
            carry_ref[i] = carry
            worst = jnp.max(carry) if worst is None else jnp.maximum(worst, jnp.max(carry))

        def more(state):
            pj, worst = state
            return jnp.logical_and(pj >= 0, worst > SB_DONE)

        def body(state):
            pj, _ = state
            worst = None
            for i in range(SB_HG):
                z = z_ref[i]
                z_ref[i] = logits(i, qi, jnp.maximum(pj - 1, 0))
                pv, carry = _sb_step(z, v_ref[tile_rows(pj), heads[i]], tt_ref[...], carry_ref[i], None)
                acc_ref[i] = acc_ref[i] + pv
                carry_ref[i] = carry
                worst = jnp.max(carry) if worst is None else jnp.maximum(worst, jnp.max(carry))
            return pj - 1, worst

        lax.while_loop(more, body, (qi - 2, worst))
        finish(qi)
        return c

    for i in range(SB_HG):
        pv, _ = _sb_step(logits(i, 0, 0), v_ref[0:SB_TQ, heads[i]], tt_ref[...], None, causal)
        acc_ref[i] = pv
        prefetch_next_tile(i, 0)
    finish(0)
    lax.fori_loop(1, SB_NQ, tile, 0)


def _sb_attention(proj, tt, out_norm3, layer):
    group_cols = lambda first: pl.BlockSpec((SEQ, SB_GW), lambda b, h: (b, first // SB_GW + h))
    return pl.pallas_call(
        _sb_kernel,
        grid=(BATCH, SB_HEADS // SB_HG),
        in_specs=[group_cols(COL_SB_Q), group_cols(COL_SB_K), group_cols(COL_SB_V), group_cols(COL_SB_G),
                  pl.BlockSpec((2 * SB_TK, 2 * SB_TK), lambda b, h: (0, 0)),
                  pl.BlockSpec((None, 1, SB_GW), lambda b, h: (layer, 0, h))],
        out_specs=group_cols(0),
        out_shape=jax.ShapeDtypeStruct((TOKENS, SB_WIDTH), BF16),
        scratch_shapes=[pltpu.VMEM((SB_HG, SB_TQ, HEAD_DIM), F32), pltpu.VMEM((SB_HG, SB_TQ, SB_TK), F32),
                        pltpu.VMEM((SB_HG, SB_TQ, SB_TQ), F32), pltpu.VMEM((SB_HG, SB_TQ, SB_TQ), F32),
                        pltpu.VMEM((SB_HG, SB_TQ, SB_TQ), F32)],
        compiler_params=_params(("parallel", "parallel")),
        name="sb_attention",
    )(proj, proj, proj, proj, tt, out_norm3)


GLA_RB = 256
GLA_NCH = GLA_RB // GLA_CHUNK
GLA_OFF_ROWS = GLA_HEADS * GLA_NCH * GLA_SUB
GLA_PAIR = 2 * GLA_HEAD_K


def _gla_kernel(q_ref, k_ref, v_ref, r_ref, g_ref, wup_ref, bup_ref, tril_ref, e_ref, bm_ref, om_ref, gn_ref,
                o_ref, st_ref):
    @pl.when(pl.program_id(1) == 0)
    def _():
        st_ref[...] = jnp.zeros_like(st_ref)

    R, C, S = GLA_RB, GLA_CHUNK, GLA_SUB
    chunks = [slice(c * C, (c + 1) * C) for c in range(GLA_NCH)]
    lane_k = lax.broadcasted_iota(jnp.int32, (1, GLA_KEY_WIDTH), 1)
    head_masks = [(lane_k // GLA_HEAD_K == h).astype(F32) for h in range(GLA_HEADS)]
    nt = (((1,), (1,)), ((), ()))
    tn = (((0,), (0,)), ((), ()))

    q = q_ref[...].astype(F32) * (GLA_HEAD_K ** -0.5)
    k = k_ref[...].astype(F32)
    v = v_ref[...]
    logits = jnp.dot(r_ref[...], wup_ref[...], preferred_element_type=F32) + bup_ref[...]
    log_alpha = (jnp.minimum(logits, 0.0) - jnp.log(1.0 + jnp.exp(-jnp.abs(logits)))) * (1.0 / GLA_GATE_TAU)
    hi, lo = _split_bf16(log_alpha)
    tril = tril_ref[...]
    bc = (jnp.dot(tril, hi, preferred_element_type=F32)
          + jnp.dot(tril, lo, preferred_element_type=F32))
    b_last = [bc[ch.stop - 1:ch.stop, :] for ch in chunks]

    q_dec = (q * jnp.exp(bc)).astype(BF16)
    k_dec = jnp.concatenate([k[ch] * jnp.exp(b_last[c] - bc[ch]) for c, ch in enumerate(chunks)],
                            axis=0).astype(BF16)
    o_inter = [[None] * (GLA_HEADS // 2) for _ in chunks]
    for p in range(GLA_HEADS // 2):
        ks = slice(p * GLA_PAIR, (p + 1) * GLA_PAIR)
        vs = slice(p * 2 * HEAD_DIM, (p + 1) * 2 * HEAD_DIM)
        st = st_ref[p]
        for c, ch in enumerate(chunks):
            o_inter[c][p] = lax.dot_general(q_dec[ch, ks], st.astype(BF16), nt, preferred_element_type=F32)
            upd = lax.dot_general(v[ch, vs], k_dec[ch, ks], tn, preferred_element_type=F32)
            st = st * jnp.exp(b_last[c][:, ks]) + upd * bm_ref[...]
        st_ref[p] = st
    o = jnp.concatenate([jnp.concatenate(parts, axis=1) for parts in o_inter], axis=0)

    off = []
    for i in range(1, C // S):
        q_parts, k_parts = [], []
        for ch in chunks:
            lo_r = ch.start + i * S
            b_row = bc[lo_r - 1:lo_r, :]
            q_parts.append(q[lo_r:lo_r + S, :] * jnp.exp(bc[lo_r:lo_r + S, :] - b_row))
            k_parts.append(k[ch] * jnp.exp(jnp.minimum(b_row - bc[ch], 0.0)))
        q_i = jnp.concatenate(q_parts, axis=0)
        k_i = jnp.concatenate(k_parts, axis=0).astype(BF16)
        lhs = jnp.concatenate([q_i * head_masks[h] for h in range(GLA_HEADS)], axis=0).astype(BF16)
        off.append(lax.dot_general(lhs, k_i, nt, preferred_element_type=F32) * om_ref[i - 1])

    def shifted(x, d):
        return pltpu.roll(x.reshape(R // S, S, GLA_KEY_WIDTH), d, 1).reshape(R, GLA_KEY_WIDTH)

    tmod = lax.broadcasted_iota(jnp.int32, (R, GLA_KEY_WIDTH), 0) % S
    diag = [(q * k).astype(BF16)]
    for d in range(1, S):
        prod = q * shifted(k, d) * jnp.exp(bc - shifted(bc, d))
        diag.append(jnp.where(tmod >= d, prod, 0.0).astype(BF16))
    score = jnp.dot(jnp.concatenate(diag, axis=1), e_ref[...], preferred_element_type=F32)

    lane_head = lax.broadcasted_iota(jnp.int32, (C, LANES), 1) // S
    no_diag = jnp.zeros((C, LANES), F32)
    o_intra = []
    for h in range(GLA_HEADS):
        blocks = []
        for c, ch in enumerate(chunks):
            a_diag = pltpu.roll(jnp.where(lane_head == h, score[ch], 0.0),
                                (LANES - (S - 1) - S * h + C * c) % LANES, 1, stride=1, stride_axis=0)
            halves = [no_diag] * (R // LANES)
            halves[C * c // LANES] = a_diag
            first = (h * GLA_NCH + c) * S
            a_off = jnp.concatenate([jnp.zeros((S, R), F32)] + [a[first:first + S, :] for a in off], axis=0)
            blocks.append(a_off + jnp.concatenate(halves, axis=1))
        a_h = jnp.concatenate(blocks, axis=0).astype(BF16)
        o_intra.append(jnp.dot(a_h, v[:, h * HEAD_DIM:(h + 1) * HEAD_DIM], preferred_element_type=F32))
    o = o + jnp.concatenate(o_intra, axis=1)

    g = g_ref[...].astype(F32)
    for h in range(GLA_HEADS):
        hs = slice(h * HEAD_DIM, (h + 1) * HEAD_DIM)
        y = _rms(o[:, hs], gn_ref[:, hs])
        o_ref[:, hs] = (y * _silu(g[:, hs])).astype(o_ref.dtype)


def _gla(proj, proj_b, wup_p, bup3, tril, e_mat, bmask, offmask, out_norm3, layer):
    nr = SEQ // GLA_RB
    row = lambda b, r: b * nr + r
    return pl.pallas_call(
        _gla_kernel,
        grid=(BATCH, nr),
        in_specs=[pl.BlockSpec((GLA_RB, GLA_KEY_WIDTH), lambda b, r: (row(b, r), COL_GLA_Q // GLA_KEY_WIDTH)),
                  pl.BlockSpec((GLA_RB, GLA_KEY_WIDTH), lambda b, r: (row(b, r), COL_GLA_K // GLA_KEY_WIDTH)),
                  pl.BlockSpec((GLA_RB, GLA_WIDTH), lambda b, r: (row(b, r), COL_GLA_V // GLA_WIDTH)),
                  pl.BlockSpec((GLA_RB, LANES), lambda b, r: (row(b, r), COLB_GLA_R // LANES)),
                  pl.BlockSpec((GLA_RB, GLA_WIDTH), lambda b, r: (row(b, r), COLB_GLA_G // GLA_WIDTH)),
                  pl.BlockSpec((None, LANES, GLA_KEY_WIDTH), lambda b, r: (layer, 0, 0)),
                  pl.BlockSpec((None, 1, GLA_KEY_WIDTH), lambda b, r: (layer, 0, 0)),
                  pl.BlockSpec((GLA_RB, GLA_RB), lambda b, r: (0, 0)),
                  pl.BlockSpec((GLA_SUB * GLA_KEY_WIDTH, LANES), lambda b, r: (0, 0)),
                  pl.BlockSpec((2 * HEAD_DIM, GLA_PAIR), lambda b, r: (0, 0)),
                  pl.BlockSpec((GLA_CHUNK // GLA_SUB - 1, GLA_OFF_ROWS, GLA_RB), lambda b, r: (0, 0, 0)),
                  pl.BlockSpec((None, 1, GLA_WIDTH), lambda b, r: (layer, 0, SB_WIDTH // GLA_WIDTH))],
        out_specs=pl.BlockSpec((GLA_RB, GLA_WIDTH), lambda b, r: (row(b, r), 0)),
        out_shape=jax.ShapeDtypeStruct((TOKENS, GLA_WIDTH), BF16),
        scratch_shapes=[pltpu.VMEM((GLA_HEADS // 2, 2 * HEAD_DIM, GLA_PAIR), F32)],
        compiler_params=_params(("parallel", "arbitrary")),
        name="gla",
    )(proj, proj, proj, proj_b, proj_b, wup_p, bup3, tril, e_mat, bmask, offmask, out_norm3)


SGU_RB = 512


def _sgu_kernel(u_ref, v_ref, g_ref, w_ref, bb_ref, sn_ref, gn_ref, o_ref):
    C = SGU_CHUNK
    keep = (lax.broadcasted_iota(jnp.int32, (C, C), 0) >= lax.broadcasted_iota(jnp.int32, (C, C), 1))
    w = [jnp.where(keep, w_ref[g], 0.0).astype(BF16) for g in range(SGU_GROUPS)]
    for c in range(SGU_RB // C):
        rows = slice(c * C, (c + 1) * C)
        u = _gelu_tanh(u_ref[rows, :].astype(F32))
        v = _gelu_tanh(v_ref[rows, :].astype(F32))
        vn = _rms(v, sn_ref[...]).astype(BF16)
        gate = g_ref[rows, :].astype(F32)
        for g in range(SGU_GROUPS):
            gs = slice(g * HEAD_DIM, (g + 1) * HEAD_DIM)
            mixed = jnp.dot(w[g], vn[:, gs], preferred_element_type=F32) + bb_ref[g]
            y = _rms(u[:, gs] * mixed, gn_ref[:, gs])
            o_ref[rows, gs] = (y * _silu(gate[:, gs])).astype(o_ref.dtype)


def _sgu(proj, w_sgu, b_sgu_b, sgu_norm3, out_norm3, layer):
    return pl.pallas_call(
        _sgu_kernel,
        grid=(TOKENS // SGU_RB,),
        in_specs=[pl.BlockSpec((SGU_RB, SGU_WIDTH), lambda i: (i, COLB_SGU_U // SGU_WIDTH)),
                  pl.BlockSpec((SGU_RB, SGU_WIDTH), lambda i: (i, COLB_SGU_V // SGU_WIDTH)),
                  pl.BlockSpec((SGU_RB, SGU_WIDTH), lambda i: (i, COLB_SGU_G // SGU_WIDTH)),
                  pl.BlockSpec((None, SGU_GROUPS, SGU_CHUNK, SGU_CHUNK), lambda i: (layer, 0, 0, 0)),
                  pl.BlockSpec((None, SGU_GROUPS, SGU_CHUNK, HEAD_DIM), lambda i: (layer, 0, 0, 0)),
                  pl.BlockSpec((None, 1, SGU_WIDTH), lambda i: (layer, 0, 0)),
                  pl.BlockSpec((None, 1, SGU_WIDTH), lambda i: (layer, 0, (SB_WIDTH + GLA_WIDTH) // SGU_WIDTH))],
        out_specs=pl.BlockSpec((SGU_RB, SGU_WIDTH), lambda i: (i, 0)),
        out_shape=jax.ShapeDtypeStruct((TOKENS, SGU_WIDTH), BF16),
        compiler_params=_params(("parallel",)),
        name="sgu",
    )(proj, proj, proj, w_sgu, b_sgu_b, sgu_norm3, out_norm3)


POST_TM = 512


def _post_kernel(x_ref, msb_ref, mgla_ref, msgu_ref, wout_ref, nx_ref, wxq_ref, kv_ref, wxo_ref, nn_ref,
                 *out_refs, last):
    mix = jnp.concatenate([msb_ref[...], mgla_ref[...], msgu_ref[...]], axis=1)
    x1 = x_ref[...] + jnp.dot(mix, wout_ref[...], preferred_element_type=F32)

    hx = _rms(x1, nx_ref[...]).astype(BF16)
    q = jnp.dot(hx, wxq_ref[...], preferred_element_type=F32).astype(BF16)
    scale = HEAD_DIM ** -0.5
    heads = []
    for h in range(XA_HEADS):
        hs = slice(h * HEAD_DIM, (h + 1) * HEAD_DIM)
        k_h = kv_ref[:, hs]
        v_h = kv_ref[:, XA_WIDTH + h * HEAD_DIM:XA_WIDTH + (h + 1) * HEAD_DIM]
        s = lax.dot_general(q[:, hs], k_h, (((1,), (1,)), ((), ())), preferred_element_type=F32) * scale
        e = jnp.exp(s - jnp.max(s, axis=-1, keepdims=True))
        p = e / jnp.sum(e, axis=-1, keepdims=True)
        heads.append(jnp.dot(p.astype(BF16), v_h, preferred_element_type=F32).astype(BF16))
    x2 = x1 + jnp.dot(jnp.concatenate(heads, axis=1), wxo_ref[...], preferred_element_type=F32)

    if last:
        out_refs[0][...] = _rms(x2, nn_ref[...])
    else:
        out_refs[0][...] = x2
        out_refs[1][...] = _rms(x2, nn_ref[...]).astype(BF16)


def _post(x, m_sb, m_gla, m_sgu, w_out_bf, norm_x3, w_xq_bf, kv, w_xo_bf, next_norm3, layer, next_idx, last):
    tm = POST_TM
    per_b = SEQ // tm
    row_spec = lambda width: pl.BlockSpec((tm, width), lambda i: (i, 0))
    const = lambda *shape: pl.BlockSpec((None,) + shape, lambda i: (layer,) + (0,) * len(shape),
                                        pipeline_mode=pl.Buffered(1))
    x_out = jax.ShapeDtypeStruct((TOKENS, D_MODEL), F32)
    if last:
        out_shape, out_specs = x_out, row_spec(D_MODEL)
    else:
        out_shape = (x_out, jax.ShapeDtypeStruct((TOKENS, D_MODEL), BF16))
        out_specs = (row_spec(D_MODEL), row_spec(D_MODEL))
    return pl.pallas_call(
        functools.partial(_post_kernel, last=last),
        grid=(TOKENS // tm,),
        in_specs=[row_spec(D_MODEL), row_spec(SB_WIDTH), row_spec(GLA_WIDTH), row_spec(SGU_WIDTH),
                  const(D_MODEL, D_MODEL),
                  const(1, D_MODEL),
                  const(D_MODEL, XA_WIDTH),
                  pl.BlockSpec((None, N_MEM, 2 * XA_WIDTH), lambda i: (layer, i // per_b, 0)),
                  const(XA_WIDTH, D_MODEL),
                  pl.BlockSpec((None, 1, D_MODEL), lambda i: (next_idx, 0, 0))],
        out_specs=out_specs,
        out_shape=out_shape,
        compiler_params=_params(("parallel",)),
        name="post",
    )(x, m_sb, m_gla, m_sgu, w_out_bf, norm_x3, w_xq_bf, kv, w_xo_bf, next_norm3)


def _constants():
    r = jnp.arange(2 * SB_TK)[:, None]
    c = jnp.arange(2 * SB_TK)[None, :]
    tt = jnp.where(c < SB_TK, (r % SB_TK) > c, True).astype(BF16)
    t = jnp.arange(GLA_RB)
    tril = ((t[:, None] >= t[None, :]) & (t[:, None] // GLA_CHUNK == t[None, :] // GLA_CHUNK)).astype(BF16)
    row_chunk = (jnp.arange(GLA_OFF_ROWS)[:, None] // GLA_SUB) % GLA_NCH
    offmask = jnp.stack([(row_chunk == t[None, :] // GLA_CHUNK) & (t[None, :] % GLA_CHUNK < i * GLA_SUB)
                         for i in range(1, GLA_CHUNK // GLA_SUB)]).astype(F32)
    e_row = jnp.arange(GLA_SUB * GLA_KEY_WIDTH)
    e_lane = GLA_SUB * ((e_row % GLA_KEY_WIDTH) // GLA_HEAD_K) + (GLA_SUB - 1) - e_row // GLA_KEY_WIDTH
    e_mat = (e_lane[:, None] == jnp.arange(LANES)[None, :]).astype(BF16)
    bmask = (jnp.arange(2 * HEAD_DIM)[:, None] // HEAD_DIM
             == jnp.arange(GLA_PAIR)[None, :] // GLA_HEAD_K).astype(F32)
    return tt, tril, e_mat, bmask, offmask


def kernel(x, mem, norm_mix, w_in, w_gla_gate_up, b_gla_gate, sgu_norm, w_sgu, b_sgu, out_norm, w_out,
           norm_xattn, norm_mem, w_xq, w_xkv, w_xo, final_norm):
    tt, tril, e_mat, bmask, offmask = _constants()
    w_in_t = jnp.swapaxes(w_in, 1, 2)
    wup_p = jnp.pad(w_gla_gate_up, ((0, 0), (0, LANES - GLA_GATE_RANK), (0, 0))).astype(BF16)
    w_out_bf, w_xq_bf, w_xo_bf = (w.astype(BF16) for w in (w_out, w_xq, w_xo))
    row3 = lambda a: a.reshape(a.shape[0], 1, a.shape[-1])
    norm_mix3, bup3, sgu_norm3, out_norm3 = row3(norm_mix), row3(b_gla_gate), row3(sgu_norm), row3(out_norm)
    norm_x3, norm_mem3 = row3(norm_xattn), row3(norm_mem)
    final3 = final_norm.reshape(1, 1, D_MODEL)
    b_sgu_b = jnp.broadcast_to(b_sgu[..., None], b_sgu.shape + (HEAD_DIM,))

    xf = x.reshape(TOKENS, D_MODEL)
    kv = _mem_kv(mem.reshape(BATCH * N_MEM, D_MODEL), norm_mem3, w_xkv)
    h = _norm_rows(xf, norm_mix3, 0)
    for l in range(DEPTH):
        last = l == DEPTH - 1
        proj = _in_proj_a(h, w_in_t, l)
        proj_b = _in_proj_b(h, w_in_t, l)
        m_sb = _sb_attention(proj, tt, out_norm3, l)
        m_gla = _gla(proj, proj_b, wup_p, bup3, tril, e_mat, bmask, offmask, out_norm3, l)
        m_sgu = _sgu(proj_b, w_sgu, b_sgu_b, sgu_norm3, out_norm3, l)
        nxt = final3 if last else norm_mix3
        res = _post(xf, m_sb, m_gla, m_sgu, w_out_bf, norm_x3, w_xq_bf, kv, w_xo_bf, nxt,
                    l, 0 if last else l + 1, last)
        if last:
            xf = res
        else:
            xf, h = res
    return xf.reshape(BATCH, SEQ, D_MODEL)
```

```python
import functools

import jax
import jax.numpy as jnp
from jax import lax
from jax.experimental import pallas as pl
from jax.experimental.pallas import tpu as pltpu

F32 = jnp.float32
BF16 = jnp.bfloat16

D_MODEL = 2048
BATCH = 4
SEQ = 2048
DEPTH = 4
TOKENS = BATCH * SEQ
HEAD_DIM = 128
SB_HEADS = 8
SB_WIDTH = SB_HEADS * HEAD_DIM
GLA_HEADS = 4
GLA_HEAD_K = 64
GLA_WIDTH = GLA_HEADS * HEAD_DIM
GLA_KEY_WIDTH = GLA_HEADS * GLA_HEAD_K
GLA_GATE_RANK = 16
GLA_GATE_TAU = 16.0
GLA_CHUNK = 64
GLA_SUB = 8
SGU_GROUPS = 4
SGU_WIDTH = 512
SGU_CHUNK = 128
N_MEM = 256
XA_HEADS = 4
XA_WIDTH = XA_HEADS * HEAD_DIM
EPS = 1e-6

LANES = 128

COL_SB_Q = 0
COL_SB_K = SB_WIDTH
COL_SB_V = 2 * SB_WIDTH
COL_SB_G = 3 * SB_WIDTH
COL_GLA_Q = 4 * SB_WIDTH
COL_GLA_K = COL_GLA_Q + GLA_KEY_WIDTH
COL_GLA_V = COL_GLA_K + GLA_KEY_WIDTH
PROJ_A_WIDTH = COL_GLA_V + GLA_WIDTH
COLB_GLA_G = 0
COLB_SGU_U = GLA_WIDTH
COLB_SGU_V = COLB_SGU_U + SGU_WIDTH
COLB_SGU_G = COLB_SGU_V + SGU_WIDTH
COLB_GLA_R = COLB_SGU_G + SGU_WIDTH
PROJ_B_MAIN = COLB_GLA_R
PROJ_B_WIDTH = COLB_GLA_R + LANES
ORIG_GLA_R = PROJ_A_WIDTH

VMEM_LIMIT = 56 * 1024 * 1024


def _params(semantics):
    return pltpu.CompilerParams(dimension_semantics=semantics, vmem_limit_bytes=VMEM_LIMIT)


def _rms(x, g):
    ms = jnp.mean(x * x, axis=-1, keepdims=True)
    return x * lax.rsqrt(ms + EPS) * g


def _silu(g):
    half = 0.5 * g
    return half + half * jnp.tanh(half)


def _softplus(z):
    return jnp.maximum(z, 0.0) + jnp.log(1.0 + jnp.exp(-jnp.abs(z)))


def _split_bf16(x):
    hi = x.astype(BF16)
    lo = (x - hi.astype(F32)).astype(BF16)
    return hi, lo


def _gelu_tanh(x):
    c = 0.7978845608028654
    half = 0.5 * x
    return half + half * jnp.tanh(x * (c * 0.044715 * (x * x) + c))


def _norm_kernel(x_ref, g_ref, o_ref):
    o_ref[...] = _rms(x_ref[...], g_ref[...]).astype(o_ref.dtype)


def _norm_rows(x2d, g3d, layer, tm=512):
    rows = x2d.shape[0]
    return pl.pallas_call(
        _norm_kernel,
        grid=(rows // tm,),
        in_specs=[pl.BlockSpec((tm, D_MODEL), lambda i: (i, 0)),
                  pl.BlockSpec((None, 1, D_MODEL), lambda i: (layer, 0, 0))],
        out_specs=pl.BlockSpec((tm, D_MODEL), lambda i: (i, 0)),
        out_shape=jax.ShapeDtypeStruct((rows, D_MODEL), BF16),
        compiler_params=_params(("parallel",)),
        name="rmsnorm_rows",
    )(x2d, g3d)


def _memkv_kernel(m_ref, g_ref, w_ref, o_ref):
    mn = _rms(m_ref[...], g_ref[...]).astype(BF16)
    o_ref[...] = jnp.dot(mn, w_ref[...].astype(BF16), preferred_element_type=F32).astype(o_ref.dtype)


def _mem_kv(mem2d, norm_mem3, w_xkv):
    rows = mem2d.shape[0]
    return pl.pallas_call(
        _memkv_kernel,
        grid=(DEPTH,),
        in_specs=[pl.BlockSpec((rows, D_MODEL), lambda l: (0, 0)),
                  pl.BlockSpec((None, 1, D_MODEL), lambda l: (l, 0, 0)),
                  pl.BlockSpec((None, D_MODEL, 2 * XA_WIDTH), lambda l: (l, 0, 0))],
        out_specs=pl.BlockSpec((None, rows, 2 * XA_WIDTH), lambda l: (l, 0, 0)),
        out_shape=jax.ShapeDtypeStruct((DEPTH, rows, 2 * XA_WIDTH), BF16),
        compiler_params=_params(("parallel",)),
        name="mem_kv",
    )(mem2d, norm_mem3, w_xkv)


W_CONV_ROWS = 256


def _store_transposed_bf16(wt_ref, wbf_ref, n_rows):
    def conv(i, c):
        r0 = pl.multiple_of(i * W_CONV_ROWS, W_CONV_ROWS)
        wbf_ref[:, pl.ds(r0, W_CONV_ROWS)] = wt_ref[pl.ds(r0, W_CONV_ROWS), :].T.astype(BF16)
        return c

    lax.fori_loop(0, n_rows // W_CONV_ROWS, conv, 0)


def _proj_a_kernel(h_ref, wt_ref, o_ref, wbf_ref):
    @pl.when(pl.program_id(1) == 0)
    def _():
        _store_transposed_bf16(wt_ref, wbf_ref, wt_ref.shape[0])

    o_ref[...] = jnp.dot(h_ref[...], wbf_ref[...], preferred_element_type=F32).astype(o_ref.dtype)


def _in_proj_a(h, w_in_t, layer, tm=1024, tn=1280):
    return pl.pallas_call(
        _proj_a_kernel,
        grid=(PROJ_A_WIDTH // tn, TOKENS // tm),
        in_specs=[pl.BlockSpec((tm, D_MODEL), lambda n, m: (m, 0)),
                  pl.BlockSpec((None, tn, D_MODEL), lambda n, m: (layer, n, 0))],
        out_specs=pl.BlockSpec((tm, tn), lambda n, m: (m, n)),
        out_shape=jax.ShapeDtypeStruct((TOKENS, PROJ_A_WIDTH), BF16),
        scratch_shapes=[pltpu.VMEM((D_MODEL, tn), BF16)],
        compiler_params=_params(("parallel", "arbitrary")),
        name="in_proj_a",
    )(h, w_in_t)


def _proj_b_kernel(h_ref, wt_ref, wr_ref, o_ref, wbf_ref):
    @pl.when(pl.program_id(0) == 0)
    def _():
        _store_transposed_bf16(wt_ref, wbf_ref, PROJ_B_MAIN)
        rank = jnp.concatenate([wr_ref[...], jnp.zeros((LANES - GLA_GATE_RANK, D_MODEL), F32)], axis=0)
        wbf_ref[:, PROJ_B_MAIN:] = rank.T.astype(BF16)

    o_ref[...] = jnp.dot(h_ref[...], wbf_ref[...], preferred_element_type=F32).astype(o_ref.dtype)


def _in_proj_b(h, w_in_t, layer, tm=1024):
    once = pl.Buffered(1)
    window = lambda rows, start: pl.BlockSpec((None, pl.Element(rows), pl.Element(D_MODEL)),
                                              lambda m: (layer, start, 0), pipeline_mode=once)
    return pl.pallas_call(
        _proj_b_kernel,
        grid=(TOKENS // tm,),
        in_specs=[pl.BlockSpec((tm, D_MODEL), lambda m: (m, 0)),
                  window(PROJ_B_MAIN, ORIG_GLA_R + GLA_GATE_RANK),
                  window(GLA_GATE_RANK, ORIG_GLA_R)],
        out_specs=pl.BlockSpec((tm, PROJ_B_WIDTH), lambda m: (m, 0)),
        out_shape=jax.ShapeDtypeStruct((TOKENS, PROJ_B_WIDTH), BF16),
        scratch_shapes=[pltpu.VMEM((D_MODEL, PROJ_B_WIDTH), BF16)],
        compiler_params=_params(("arbitrary",)),
        name="in_proj_b",
    )(h, w_in_t, w_in_t)


SB_TQ = 256
SB_TK = 128
SB_NQ = SEQ // SB_TQ
SB_QB = SB_TQ // SB_TK
SB_DONE = -88.0
SB_HG = 4
SB_GW = SB_HG * HEAD_DIM


def _sb_suffix(log_keep, tt):
    hi, lo = _split_bf16(log_keep)
    return jnp.dot(jnp.concatenate([hi, lo], axis=1), tt, preferred_element_type=F32)


def _sb_step(z, v, tt, carry, causal):
    sp = _softplus(z)
    log_keep = -sp if causal is None else jnp.where(causal, -sp, 0.0)
    cs_r = _sb_suffix(log_keep[:, SB_TK:], tt)
    cs_l = _sb_suffix(log_keep[:, :SB_TK], tt)
    log_beta = z - sp
    arg_r = log_beta[:, SB_TK:] + cs_r[:, :SB_TK]
    carry_l = cs_r[:, SB_TK:]
    if carry is not None:
        arg_r = arg_r + carry
        carry_l = carry_l + carry
    a = jnp.exp(jnp.concatenate([log_beta[:, :SB_TK] + cs_l[:, :SB_TK] + carry_l, arg_r], axis=1))
    if causal is not None:
        a = jnp.where(causal, a, 0.0)
    return jnp.dot(a.astype(BF16), v, preferred_element_type=F32), carry_l + cs_l[:, SB_TK:]


def _sb_block(z, tt, carry):
    sp = _softplus(z)
    cs = _sb_suffix(-sp, tt)
    return jnp.exp((z - sp) + cs[:, :SB_TK] + carry), carry + cs[:, SB_TK:]


def _sb_kernel(q_ref, k_ref, v_ref, g_ref, tt_ref, gn_ref, o_ref, acc_ref, carry_ref, zd_ref, z1_ref):
    scale = HEAD_DIM ** -0.5
    nt = (((1,), (1,)), ((), ()))
    heads = [slice(i * HEAD_DIM, (i + 1) * HEAD_DIM) for i in range(SB_HG)]
    causal = (lax.broadcasted_iota(jnp.int32, (SB_TQ, SB_TQ), 1)
              < lax.broadcasted_iota(jnp.int32, (SB_TQ, SB_TQ), 0))

    def key_rows(first_block, n_blocks):
        if isinstance(first_block, int):
            return slice(first_block * SB_TK, (first_block + n_blocks) * SB_TK)
        return pl.ds(pl.multiple_of(first_block * SB_TK, SB_TK), n_blocks * SB_TK)

    def tile_rows(t):
        return key_rows(t * SB_QB, SB_QB)

    def logits(i, qt, first_block, n_blocks):
        return lax.dot_general(q_ref[tile_rows(qt), heads[i]], k_ref[key_rows(first_block, n_blocks), heads[i]],
                               nt, preferred_element_type=F32) * scale

    def finish(qi, which):
        for i in which:
            y = _rms(acc_ref[i], gn_ref[:, heads[i]])
            gate = g_ref[tile_rows(qi), heads[i]].astype(F32)
            o_ref[tile_rows(qi), heads[i]] = (y * _silu(gate)).astype(o_ref.dtype)

    def prefetch_next_tile(i, qi):
        nxt = jnp.minimum(qi + 1, SB_NQ - 1)
        zd_ref[i] = logits(i, nxt, nxt * SB_QB, SB_QB)
        z1_ref[i] = logits(i, nxt, nxt * SB_QB - 1, 1)

    def tile(qi, c):
        worst = []
        for i in range(SB_HG):
            zd, z1 = zd_ref[i], z1_ref[i]
            prefetch_next_tile(i, qi)
            pv, carry = _sb_step(zd, v_ref[tile_rows(qi), heads[i]], tt_ref[...], None, causal)
            a, carry = _sb_block(z1, tt_ref[...], carry)
            acc_ref[i] = pv + jnp.dot(a.astype(BF16), v_ref[key_rows(qi * SB_QB - 1, 1), heads[i]],
                                      preferred_element_type=F32)
            carry_ref[i] = carry
            worst.append(jnp.max(carry))

        def more(state):
            kb, *left = state
            return jnp.logical_and(kb >= 0, functools.reduce(jnp.maximum, left) > SB_DONE)

        def block_step(state):
            kb, *left = state

            def head_step(i):
                a, carry = _sb_block(logits(i, qi, kb, 1), tt_ref[...], carry_ref[i])
                acc_ref[i] = acc_ref[i] + jnp.dot(a.astype(BF16), v_ref[key_rows(kb, 1), heads[i]],
                                                  preferred_element_type=F32)
                carry_ref[i] = carry
                return jnp.max(carry)

            return (kb - 1, *[lax.cond(left[i] > SB_DONE, functools.partial(head_step, i), lambda i=i: left[i])
                              for i in range(SB_HG)])

        finish(qi, range(SB_HG))
        lax.while_loop(more, block_step, (qi * SB_QB - 2, *worst))
        for i in range(SB_HG):
            @pl.when(worst[i] > SB_DONE)
            def _(i=i):
                finish(qi, [i])

        return c

    for i in range(SB_HG):
        pv, _ = _sb_step(logits(i, 0, 0, SB_QB), v_ref[0:SB_TQ, heads[i]], tt_ref[...], None, causal)
        acc_ref[i] = pv
        prefetch_next_tile(i, 0)
    finish(0, range(SB_HG))
    lax.fori_loop(1, SB_NQ, tile, 0)


def _sb_attention(proj, tt, out_norm3, layer):
    group_cols = lambda first: pl.BlockSpec((SEQ, SB_GW), lambda b, h: (b, first // SB_GW + h))
    return pl.pallas_call(
        _sb_kernel,
        grid=(BATCH, SB_HEADS // SB_HG),
        in_specs=[group_cols(COL_SB_Q), group_cols(COL_SB_K), group_cols(COL_SB_V), group_cols(COL_SB_G),
                  pl.BlockSpec((2 * SB_TK, 2 * SB_TK), lambda b, h: (0, 0)),
                  pl.BlockSpec((None, 1, SB_GW), lambda b, h: (layer, 0, h))],
        out_specs=group_cols(0),
        out_shape=jax.ShapeDtypeStruct((TOKENS, SB_WIDTH), BF16),
        scratch_shapes=[pltpu.VMEM((SB_HG, SB_TQ, HEAD_DIM), F32), pltpu.VMEM((SB_HG, SB_TQ, SB_TK), F32),
                        pltpu.VMEM((SB_HG, SB_TQ, SB_TQ), F32), pltpu.VMEM((SB_HG, SB_TQ, SB_TK), F32)],
        compiler_params=_params(("parallel", "parallel")),
        name="sb_attention",
    )(proj, proj, proj, proj, tt, out_norm3)


GLA_RB = 256
GLA_NCH = GLA_RB // GLA_CHUNK
GLA_OFF_ROWS = GLA_HEADS * GLA_NCH * GLA_SUB
GLA_PAIR = 2 * GLA_HEAD_K


def _gla_kernel(q_ref, k_ref, v_ref, r_ref, g_ref, wup_ref, bup_ref, tril_ref, e_ref, bm_ref, om_ref, gn_ref,
                o_ref, st_ref):
    @pl.when(pl.program_id(1) == 0)
    def _():
        st_ref[...] = jnp.zeros_like(st_ref)

    R, C, S = GLA_RB, GLA_CHUNK, GLA_SUB
    chunks = [slice(c * C, (c + 1) * C) for c in range(GLA_NCH)]
    lane_k = lax.broadcasted_iota(jnp.int32, (1, GLA_KEY_WIDTH), 1)
    head_masks = [(lane_k // GLA_HEAD_K == h).astype(F32) for h in range(GLA_HEADS)]
    nt = (((1,), (1,)), ((), ()))
    tn = (((0,), (0,)), ((), ()))

    q = q_ref[...].astype(F32) * (GLA_HEAD_K ** -0.5)
    k = k_ref[...].astype(F32)
    v = v_ref[...]
    logits = jnp.dot(r_ref[...], wup_ref[...], preferred_element_type=F32) + bup_ref[...]
    log_alpha = (jnp.minimum(logits, 0.0) - jnp.log(1.0 + jnp.exp(-jnp.abs(logits)))) * (1.0 / GLA_GATE_TAU)
    hi, lo = _split_bf16(log_alpha)
    tril = tril_ref[...]
    bc = (jnp.dot(tril, hi, preferred_element_type=F32)
          + jnp.dot(tril, lo, preferred_element_type=F32))
    b_last = [bc[ch.stop - 1:ch.stop, :] for ch in chunks]

    q_dec = (q * jnp.exp(bc)).astype(BF16)
    k_dec = jnp.concatenate([k[ch] * jnp.exp(b_last[c] - bc[ch]) for c, ch in enumerate(chunks)],
                            axis=0).astype(BF16)
    o_inter = [[None] * (GLA_HEADS // 2) for _ in chunks]
    for p in range(GLA_HEADS // 2):
        ks = slice(p * GLA_PAIR, (p + 1) * GLA_PAIR)
        vs = slice(p * 2 * HEAD_DIM, (p + 1) * 2 * HEAD_DIM)
        st = st_ref[p]
        for c, ch in enumerate(chunks):
            o_inter[c][p] = lax.dot_general(q_dec[ch, ks], st.astype(BF16), nt, preferred_element_type=F32)
            upd = lax.dot_general(v[ch, vs], k_dec[ch, ks], tn, preferred_element_type=F32)
            st = st * jnp.exp(b_last[c][:, ks]) + upd * bm_ref[...]
        st_ref[p] = st
    o = jnp.concatenate([jnp.concatenate(parts, axis=1) for parts in o_inter], axis=0)

    off = []
    for i in range(1, C // S):
        q_parts, k_parts = [], []
        for ch in chunks:
            lo_r = ch.start + i * S
            b_row = bc[lo_r - 1:lo_r, :]
            q_parts.append(q[lo_r:lo_r + S, :] * jnp.exp(bc[lo_r:lo_r + S, :] - b_row))
            k_parts.append(k[ch] * jnp.exp(jnp.minimum(b_row - bc[ch], 0.0)))
        q_i = jnp.concatenate(q_parts, axis=0)
        k_i = jnp.concatenate(k_parts, axis=0).astype(BF16)
        lhs = jnp.concatenate([q_i * head_masks[h] for h in range(GLA_HEADS)], axis=0).astype(BF16)
        off.append(lax.dot_general(lhs, k_i, nt, preferred_element_type=F32) * om_ref[i - 1])

    def shifted(x, d):
        return pltpu.roll(x.reshape(R // S, S, GLA_KEY_WIDTH), d, 1).reshape(R, GLA_KEY_WIDTH)

    tmod = lax.broadcasted_iota(jnp.int32, (R, GLA_KEY_WIDTH), 0) % S
    diag = [(q * k).astype(BF16)]
    for d in range(1, S):
        prod = q * shifted(k, d) * jnp.exp(bc - shifted(bc, d))
        diag.append(jnp.where(tmod >= d, prod, 0.0).astype(BF16))
    score = jnp.dot(jnp.concatenate(diag, axis=1), e_ref[...], preferred_element_type=F32)

    lane_head = lax.broadcasted_iota(jnp.int32, (C, LANES), 1) // S
    no_diag = jnp.zeros((C, LANES), F32)
    o_intra = []
    for h in range(GLA_HEADS):
        blocks = []
        for c, ch in enumerate(chunks):
            a_diag = pltpu.roll(jnp.where(lane_head == h, score[ch], 0.0),
                                (LANES - (S - 1) - S * h + C * c) % LANES, 1, stride=1, stride_axis=0)
            halves = [no_diag] * (R // LANES)
            halves[C * c // LANES] = a_diag
            first = (h * GLA_NCH + c) * S
            a_off = jnp.concatenate([jnp.zeros((S, R), F32)] + [a[first:first + S, :] for a in off], axis=0)
            blocks.append(a_off + jnp.concatenate(halves, axis=1))
        a_h = jnp.concatenate(blocks, axis=0).astype(BF16)
        o_intra.append(jnp.dot(a_h, v[:, h * HEAD_DIM:(h + 1) * HEAD_DIM], preferred_element_type=F32))
    o = o + jnp.concatenate(o_intra, axis=1)

    g = g_ref[...].astype(F32)
    for h in range(GLA_HEADS):
        hs = slice(h * HEAD_DIM, (h + 1) * HEAD_DIM)
        y = _rms(o[:, hs], gn_ref[:, hs])
        o_ref[:, hs] = (y * _silu(g[:, hs])).astype(o_ref.dtype)


def _gla(proj, proj_b, wup_p, bup3, tril, e_mat, bmask, offmask, out_norm3, layer):
    nr = SEQ // GLA_RB
    row = lambda b, r: b * nr + r
    return pl.pallas_call(
        _gla_kernel,
        grid=(BATCH, nr),
        in_specs=[pl.BlockSpec((GLA_RB, GLA_KEY_WIDTH), lambda b, r: (row(b, r), COL_GLA_Q // GLA_KEY_WIDTH)),
                  pl.BlockSpec((GLA_RB, GLA_KEY_WIDTH), lambda b, r: (row(b, r), COL_GLA_K // GLA_KEY_WIDTH)),
                  pl.BlockSpec((GLA_RB, GLA_WIDTH), lambda b, r: (row(b, r), COL_GLA_V // GLA_WIDTH)),
                  pl.BlockSpec((GLA_RB, LANES), lambda b, r: (row(b, r), COLB_GLA_R // LANES)),
                  pl.BlockSpec((GLA_RB, GLA_WIDTH), lambda b, r: (row(b, r), COLB_GLA_G // GLA_WIDTH)),
                  pl.BlockSpec((None, LANES, GLA_KEY_WIDTH), lambda b, r: (layer, 0, 0)),
                  pl.BlockSpec((None, 1, GLA_KEY_WIDTH), lambda b, r: (layer, 0, 0)),
                  pl.BlockSpec((GLA_RB, GLA_RB), lambda b, r: (0, 0)),
                  pl.BlockSpec((GLA_SUB * GLA_KEY_WIDTH, LANES), lambda b, r: (0, 0)),
                  pl.BlockSpec((2 * HEAD_DIM, GLA_PAIR), lambda b, r: (0, 0)),
                  pl.BlockSpec((GLA_CHUNK // GLA_SUB - 1, GLA_OFF_ROWS, GLA_RB), lambda b, r: (0, 0, 0)),
                  pl.BlockSpec((None, 1, GLA_WIDTH), lambda b, r: (layer, 0, SB_WIDTH // GLA_WIDTH))],
        out_specs=pl.BlockSpec((GLA_RB, GLA_WIDTH), lambda b, r: (row(b, r), 0)),
        out_shape=jax.ShapeDtypeStruct((TOKENS, GLA_WIDTH), BF16),
        scratch_shapes=[pltpu.VMEM((GLA_HEADS // 2, 2 * HEAD_DIM, GLA_PAIR), F32)],
        compiler_params=_params(("parallel", "arbitrary")),
        name="gla",
    )(proj, proj, proj, proj_b, proj_b, wup_p, bup3, tril, e_mat, bmask, offmask, out_norm3)


SGU_RB = 512


def _sgu_kernel(u_ref, v_ref, g_ref, w_ref, bb_ref, sn_ref, gn_ref, o_ref):
    C = SGU_CHUNK
    keep = (lax.broadcasted_iota(jnp.int32, (C, C), 0) >= lax.broadcasted_iota(jnp.int32, (C, C), 1))
    w = [jnp.where(keep, w_ref[g], 0.0).astype(BF16) for g in range(SGU_GROUPS)]
    for c in range(SGU_RB // C):
        rows = slice(c * C, (c + 1) * C)
        u = _gelu_tanh(u_ref[rows, :].astype(F32))
        v = _gelu_tanh(v_ref[rows, :].astype(F32))
        vn = _rms(v, sn_ref[...]).astype(BF16)
        gate = g_ref[rows, :].astype(F32)
        for g in range(SGU_GROUPS):
            gs = slice(g * HEAD_DIM, (g + 1) * HEAD_DIM)
            mixed = jnp.dot(w[g], vn[:, gs], preferred_element_type=F32) + bb_ref[g]
            y = _rms(u[:, gs] * mixed, gn_ref[:, gs])
            o_ref[rows, gs] = (y * _silu(gate[:, gs])).astype(o_ref.dtype)


def _sgu(proj, w_sgu, b_sgu_b, sgu_norm3, out_norm3, layer):
    return pl.pallas_call(
        _sgu_kernel,
        grid=(TOKENS // SGU_RB,),
        in_specs=[pl.BlockSpec((SGU_RB, SGU_WIDTH), lambda i: (i, COLB_SGU_U // SGU_WIDTH)),
                  pl.BlockSpec((SGU_RB, SGU_WIDTH), lambda i: (i, COLB_SGU_V // SGU_WIDTH)),
                  pl.BlockSpec((SGU_RB, SGU_WIDTH), lambda i: (i, COLB_SGU_G // SGU_WIDTH)),
                  pl.BlockSpec((None, SGU_GROUPS, SGU_CHUNK, SGU_CHUNK), lambda i: (layer, 0, 0, 0)),
                  pl.BlockSpec((None, SGU_GROUPS, SGU_CHUNK, HEAD_DIM), lambda i: (layer, 0, 0, 0)),
                  pl.BlockSpec((None, 1, SGU_WIDTH), lambda i: (layer, 0, 0)),
                  pl.BlockSpec((None, 1, SGU_WIDTH), lambda i: (layer, 0, (SB_WIDTH + GLA_WIDTH) // SGU_WIDTH))],
        out_specs=pl.BlockSpec((SGU_RB, SGU_WIDTH), lambda i: (i, 0)),
        out_shape=jax.ShapeDtypeStruct((TOKENS, SGU_WIDTH), BF16),
        compiler_params=_params(("parallel",)),
        name="sgu",
    )(proj, proj, proj, w_sgu, b_sgu_b, sgu_norm3, out_norm3)


POST_TM = 512


def _post_kernel(x_ref, msb_ref, mgla_ref, msgu_ref, wout_ref, nx_ref, wxq_ref, kv_ref, wxo_ref, nn_ref,
                 *out_refs, last):
    mix = jnp.concatenate([msb_ref[...], mgla_ref[...], msgu_ref[...]], axis=1)
    x1 = x_ref[...] + jnp.dot(mix, wout_ref[...], preferred_element_type=F32)

    hx = _rms(x1, nx_ref[...]).astype(BF16)
    q = jnp.dot(hx, wxq_ref[...], preferred_element_type=F32).astype(BF16)
    scale = HEAD_DIM ** -0.5
    heads = []
    for h in range(XA_HEADS):
        hs = slice(h * HEAD_DIM, (h + 1) * HEAD_DIM)
        k_h = kv_ref[:, hs]
        v_h = kv_ref[:, XA_WIDTH + h * HEAD_DIM:XA_WIDTH + (h + 1) * HEAD_DIM]
        s = lax.dot_general(q[:, hs], k_h, (((1,), (1,)), ((), ())), preferred_element_type=F32) * scale
        e = jnp.exp(s - jnp.max(s, axis=-1, keepdims=True))
        p = e * (1.0 / jnp.sum(e, axis=-1, keepdims=True))
        heads.append(jnp.dot(p.astype(BF16), v_h, preferred_element_type=F32).astype(BF16))
    x2 = x1 + jnp.dot(jnp.concatenate(heads, axis=1), wxo_ref[...], preferred_element_type=F32)

    if last:
        out_refs[0][...] = _rms(x2, nn_ref[...])
    else:
        out_refs[0][...] = x2
        out_refs[1][...] = _rms(x2, nn_ref[...]).astype(BF16)


def _post(x, m_sb, m_gla, m_sgu, w_out_bf, norm_x3, w_xq_bf, kv, w_xo_bf, next_norm3, layer, next_idx, last):
    tm = POST_TM
    per_b = SEQ // tm
    row_spec = lambda width: pl.BlockSpec((tm, width), lambda i: (i, 0))
    const = lambda *shape: pl.BlockSpec((None,) + shape, lambda i: (layer,) + (0,) * len(shape),
                                        pipeline_mode=pl.Buffered(1))
    x_out = jax.ShapeDtypeStruct((TOKENS, D_MODEL), F32)
    if last:
        out_shape, out_specs = x_out, row_spec(D_MODEL)
    else:
        out_shape = (x_out, jax.ShapeDtypeStruct((TOKENS, D_MODEL), BF16))
        out_specs = (row_spec(D_MODEL), row_spec(D_MODEL))
    return pl.pallas_call(
        functools.partial(_post_kernel, last=last),
        grid=(TOKENS // tm,),
        in_specs=[row_spec(D_MODEL), row_spec(SB_WIDTH), row_spec(GLA_WIDTH), row_spec(SGU_WIDTH),
                  const(D_MODEL, D_MODEL),
                  const(1, D_MODEL),
                  const(D_MODEL, XA_WIDTH),
                  pl.BlockSpec((None, N_MEM, 2 * XA_WIDTH), lambda i: (layer, i // per_b, 0)),
                  const(XA_WIDTH, D_MODEL),
                  pl.BlockSpec((None, 1, D_MODEL), lambda i: (next_idx, 0, 0))],
        out_specs=out_specs,
        out_shape=out_shape,
        compiler_params=_params(("parallel",)),
        name="post",
    )(x, m_sb, m_gla, m_sgu, w_out_bf, norm_x3, w_xq_bf, kv, w_xo_bf, next_norm3)


def _constants():
    r = jnp.arange(2 * SB_TK)[:, None]
    c = jnp.arange(2 * SB_TK)[None, :]
    tt = jnp.where(c < SB_TK, (r % SB_TK) > c, True).astype(BF16)
    t = jnp.arange(GLA_RB)
    tril = ((t[:, None] >= t[None, :]) & (t[:, None] // GLA_CHUNK == t[None, :] // GLA_CHUNK)).astype(BF16)
    row_chunk = (jnp.arange(GLA_OFF_ROWS)[:, None] // GLA_SUB) % GLA_NCH
    offmask = jnp.stack([(row_chunk == t[None, :] // GLA_CHUNK) & (t[None, :] % GLA_CHUNK < i * GLA_SUB)
                         for i in range(1, GLA_CHUNK // GLA_SUB)]).astype(F32)
    e_row = jnp.arange(GLA_SUB * GLA_KEY_WIDTH)
    e_lane = GLA_SUB * ((e_row % GLA_KEY_WIDTH) // GLA_HEAD_K) + (GLA_SUB - 1) - e_row // GLA_KEY_WIDTH
    e_mat = (e_lane[:, None] == jnp.arange(LANES)[None, :]).astype(BF16)
    bmask = (jnp.arange(2 * HEAD_DIM)[:, None] // HEAD_DIM
             == jnp.arange(GLA_PAIR)[None, :] // GLA_HEAD_K).astype(F32)
    return tt, tril, e_mat, bmask, offmask


def kernel(x, mem, norm_mix, w_in, w_gla_gate_up, b_gla_gate, sgu_norm, w_sgu, b_sgu, out_norm, w_out,
           norm_xattn, norm_mem, w_xq, w_xkv, w_xo, final_norm):
    tt, tril, e_mat, bmask, offmask = _constants()
    w_in_t = jnp.swapaxes(w_in, 1, 2)
    wup_p = jnp.pad(w_gla_gate_up, ((0, 0), (0, LANES - GLA_GATE_RANK), (0, 0))).astype(BF16)
    w_out_bf, w_xq_bf, w_xo_bf = (w.astype(BF16) for w in (w_out, w_xq, w_xo))
    row3 = lambda a: a.reshape(a.shape[0], 1, a.shape[-1])
    norm_mix3, bup3, sgu_norm3, out_norm3 = row3(norm_mix), row3(b_gla_gate), row3(sgu_norm), row3(out_norm)
    norm_x3, norm_mem3 = row3(norm_xattn), row3(norm_mem)
    final3 = final_norm.reshape(1, 1, D_MODEL)
    b_sgu_b = jnp.broadcast_to(b_sgu[..., None], b_sgu.shape + (HEAD_DIM,))

    xf = x.reshape(TOKENS, D_MODEL)
    kv = _mem_kv(mem.reshape(BATCH * N_MEM, D_MODEL), norm_mem3, w_xkv)
    h = _norm_rows(xf, norm_mix3, 0)
    for l in range(DEPTH):
        last = l == DEPTH - 1
        proj = _in_proj_a(h, w_in_t, l)
        proj_b = _in_proj_b(h, w_in_t, l)
        m_sb = _sb_attention(proj, tt, out_norm3, l)
        m_gla = _gla(proj, proj_b, wup_p, bup3, tril, e_mat, bmask, offmask, out_norm3, l)
        m_sgu = _sgu(proj_b, w_sgu, b_sgu_b, sgu_norm3, out_norm3, l)
        nxt = final3 if last else norm_mix3
        res = _post(xf, m_sb, m_gla, m_sgu, w_out_bf, norm_x3, w_xq_bf, kv, w_xo_bf, nxt,
                    l, 0 if last else l + 1, last)
        if last:
            xf = res
        else:
            xf, h = res
    return xf.reshape(BATCH, SEQ, D_MODEL)
```

```python
import functools

import jax
import jax.numpy as jnp
from jax import lax
from jax.experimental import pallas as pl
from jax.experimental.pallas import tpu as pltpu

F32 = jnp.float32
BF16 = jnp.bfloat16

D_MODEL = 2048
BATCH = 4
SEQ = 2048
DEPTH = 4
TOKENS = BATCH * SEQ
HEAD_DIM = 128
SB_HEADS = 8
SB_WIDTH = SB_HEADS * HEAD_DIM
GLA_HEADS = 4
GLA_HEAD_K = 64
GLA_WIDTH = GLA_HEADS * HEAD_DIM
GLA_KEY_WIDTH = GLA_HEADS * GLA_HEAD_K
GLA_GATE_RANK = 16
GLA_GATE_TAU = 16.0
GLA_CHUNK = 64
GLA_SUB = 8
SGU_GROUPS = 4
SGU_WIDTH = 512
SGU_CHUNK = 128
N_MEM = 256
XA_HEADS = 4
XA_WIDTH = XA_HEADS * HEAD_DIM
EPS = 1e-6

LANES = 128

COL_SB_Q = 0
COL_SB_K = SB_WIDTH
COL_SB_V = 2 * SB_WIDTH
COL_SB_G = 3 * SB_WIDTH
COL_GLA_Q = 4 * SB_WIDTH
COL_GLA_K = COL_GLA_Q + GLA_KEY_WIDTH
COL_GLA_V = COL_GLA_K + GLA_KEY_WIDTH
PROJ_A_WIDTH = COL_GLA_V + GLA_WIDTH
COLB_GLA_G = 0
COLB_SGU_U = GLA_WIDTH
COLB_SGU_V = COLB_SGU_U + SGU_WIDTH
COLB_SGU_G = COLB_SGU_V + SGU_WIDTH
COLB_GLA_R = COLB_SGU_G + SGU_WIDTH
PROJ_B_MAIN = COLB_GLA_R
PROJ_B_WIDTH = COLB_GLA_R + LANES
ORIG_GLA_R = PROJ_A_WIDTH

VMEM_LIMIT = 56 * 1024 * 1024


def _params(semantics):
    return pltpu.CompilerParams(dimension_semantics=semantics, vmem_limit_bytes=VMEM_LIMIT)


def _rms(x, g):
    ms = jnp.mean(x * x, axis=-1, keepdims=True)
    return x * lax.rsqrt(ms + EPS) * g


def _silu(g):
    half = 0.5 * g
    return half + half * jnp.tanh(half)


def _softplus(z):
    return jnp.maximum(z, 0.0) + jnp.log(1.0 + jnp.exp(-jnp.abs(z)))


def _split_bf16(x):
    hi = x.astype(BF16)
    lo = (x - hi.astype(F32)).astype(BF16)
    return hi, lo


def _gelu_tanh(x):
    c = 0.7978845608028654
    half = 0.5 * x
    return half + half * jnp.tanh(x * (c * 0.044715 * (x * x) + c))


def _norm_kernel(x_ref, g_ref, o_ref):
    o_ref[...] = _rms(x_ref[...], g_ref[...]).astype(o_ref.dtype)


def _norm_rows(x2d, g3d, layer, tm=512):
    rows = x2d.shape[0]
    return pl.pallas_call(
        _norm_kernel,
        grid=(rows // tm,),
        in_specs=[pl.BlockSpec((tm, D_MODEL), lambda i: (i, 0)),
                  pl.BlockSpec((None, 1, D_MODEL), lambda i: (layer, 0, 0))],
        out_specs=pl.BlockSpec((tm, D_MODEL), lambda i: (i, 0)),
        out_shape=jax.ShapeDtypeStruct((rows, D_MODEL), BF16),
        compiler_params=_params(("parallel",)),
        name="rmsnorm_rows",
    )(x2d, g3d)


def _memkv_kernel(m_ref, g_ref, w_ref, o_ref):
    mn = _rms(m_ref[...], g_ref[...]).astype(BF16)
    o_ref[...] = jnp.dot(mn, w_ref[...].astype(BF16), preferred_element_type=F32).astype(o_ref.dtype)


def _mem_kv(mem2d, norm_mem3, w_xkv):
    rows = mem2d.shape[0]
    return pl.pallas_call(
        _memkv_kernel,
        grid=(DEPTH,),
        in_specs=[pl.BlockSpec((rows, D_MODEL), lambda l: (0, 0)),
                  pl.BlockSpec((None, 1, D_MODEL), lambda l: (l, 0, 0)),
                  pl.BlockSpec((None, D_MODEL, 2 * XA_WIDTH), lambda l: (l, 0, 0))],
        out_specs=pl.BlockSpec((None, rows, 2 * XA_WIDTH), lambda l: (l, 0, 0)),
        out_shape=jax.ShapeDtypeStruct((DEPTH, rows, 2 * XA_WIDTH), BF16),
        compiler_params=_params(("parallel",)),
        name="mem_kv",
    )(mem2d, norm_mem3, w_xkv)


W_CONV_ROWS = 256


def _store_transposed_bf16(wt_ref, wbf_ref, n_rows):
    def conv(i, c):
        r0 = pl.multiple_of(i * W_CONV_ROWS, W_CONV_ROWS)
        wbf_ref[:, pl.ds(r0, W_CONV_ROWS)] = wt_ref[pl.ds(r0, W_CONV_ROWS), :].T.astype(BF16)
        return c

    lax.fori_loop(0, n_rows // W_CONV_ROWS, conv, 0)


def _proj_a_kernel(h_ref, wt_ref, o_ref, wbf_ref):
    @pl.when(pl.program_id(1) == 0)
    def _():
        _store_transposed_bf16(wt_ref, wbf_ref, wt_ref.shape[0])

    o_ref[...] = jnp.dot(h_ref[...], wbf_ref[...], preferred_element_type=F32).astype(o_ref.dtype)


def _in_proj_a(h, w_in_t, layer, tm=1024, tn=1280):
    return pl.pallas_call(
        _proj_a_kernel,
        grid=(PROJ_A_WIDTH // tn, TOKENS // tm),
        in_specs=[pl.BlockSpec((tm, D_MODEL), lambda n, m: (m, 0)),
                  pl.BlockSpec((None, tn, D_MODEL), lambda n, m: (layer, n, 0))],
        out_specs=pl.BlockSpec((tm, tn), lambda n, m: (m, n)),
        out_shape=jax.ShapeDtypeStruct((TOKENS, PROJ_A_WIDTH), BF16),
        scratch_shapes=[pltpu.VMEM((D_MODEL, tn), BF16)],
        compiler_params=_params(("parallel", "arbitrary")),
        name="in_proj_a",
    )(h, w_in_t)


def _proj_b_kernel(h_ref, wt_ref, wr_ref, o_ref, wbf_ref):
    @pl.when(pl.program_id(0) == 0)
    def _():
        _store_transposed_bf16(wt_ref, wbf_ref, PROJ_B_MAIN)
        rank = jnp.concatenate([wr_ref[...], jnp.zeros((LANES - GLA_GATE_RANK, D_MODEL), F32)], axis=0)
        wbf_ref[:, PROJ_B_MAIN:] = rank.T.astype(BF16)

    o_ref[...] = jnp.dot(h_ref[...], wbf_ref[...], preferred_element_type=F32).astype(o_ref.dtype)


def _in_proj_b(h, w_in_t, layer, tm=1024):
    once = pl.Buffered(1)
    window = lambda rows, start: pl.BlockSpec((None, pl.Element(rows), pl.Element(D_MODEL)),
                                              lambda m: (layer, start, 0), pipeline_mode=once)
    return pl.pallas_call(
        _proj_b_kernel,
        grid=(TOKENS // tm,),
        in_specs=[pl.BlockSpec((tm, D_MODEL), lambda m: (m, 0)),
                  window(PROJ_B_MAIN, ORIG_GLA_R + GLA_GATE_RANK),
                  window(GLA_GATE_RANK, ORIG_GLA_R)],
        out_specs=pl.BlockSpec((tm, PROJ_B_WIDTH), lambda m: (m, 0)),
        out_shape=jax.ShapeDtypeStruct((TOKENS, PROJ_B_WIDTH), BF16),
        scratch_shapes=[pltpu.VMEM((D_MODEL, PROJ_B_WIDTH), BF16)],
        compiler_params=_params(("arbitrary",)),
        name="in_proj_b",
    )(h, w_in_t, w_in_t)


SB_TQ = 256
SB_TK = 128
SB_NQ = SEQ // SB_TQ
SB_QB = SB_TQ // SB_TK
SB_DONE = -88.0
SB_HG = 4
SB_GW = SB_HG * HEAD_DIM


def _sb_suffix(log_keep, tt):
    hi, lo = _split_bf16(log_keep)
    return jnp.dot(jnp.concatenate([hi, lo], axis=1), tt, preferred_element_type=F32)


def _sb_step(z, v, tt, carry, causal):
    sp = _softplus(z)
    log_keep = -sp if causal is None else jnp.where(causal, -sp, 0.0)
    cs_r = _sb_suffix(log_keep[:, SB_TK:], tt)
    cs_l = _sb_suffix(log_keep[:, :SB_TK], tt)
    log_beta = z - sp
    arg_r = log_beta[:, SB_TK:] + cs_r[:, :SB_TK]
    carry_l = cs_r[:, SB_TK:]
    if carry is not None:
        arg_r = arg_r + carry
        carry_l = carry_l + carry
    a = jnp.exp(jnp.concatenate([log_beta[:, :SB_TK] + cs_l[:, :SB_TK] + carry_l, arg_r], axis=1))
    if causal is not None:
        a = jnp.where(causal, a, 0.0)
    return jnp.dot(a.astype(BF16), v, preferred_element_type=F32), carry_l + cs_l[:, SB_TK:]


def _sb_block(z, tt, carry):
    sp = _softplus(z)
    cs = _sb_suffix(-sp, tt)
    return jnp.exp((z - sp) + cs[:, :SB_TK] + carry), carry + cs[:, SB_TK:]


def _sb_kernel(q_ref, k_ref, v_ref, g_ref, tt_ref, gn_ref, o_ref, acc_ref, carry_ref, zd_ref, z1_ref):
    scale = HEAD_DIM ** -0.5
    nt = (((1,), (1,)), ((), ()))
    heads = [slice(i * HEAD_DIM, (i + 1) * HEAD_DIM) for i in range(SB_HG)]
    causal = (lax.broadcasted_iota(jnp.int32, (SB_TQ, SB_TQ), 1)
              < lax.broadcasted_iota(jnp.int32, (SB_TQ, SB_TQ), 0))

    def key_rows(first_block, n_blocks):
        if isinstance(first_block, int):
            return slice(first_block * SB_TK, (first_block + n_blocks) * SB_TK)
        return pl.ds(pl.multiple_of(first_block * SB_TK, SB_TK), n_blocks * SB_TK)

    def tile_rows(t):
        return key_rows(t * SB_QB, SB_QB)

    def logits(i, qt, first_block, n_blocks):
        return lax.dot_general(q_ref[tile_rows(qt), heads[i]], k_ref[key_rows(first_block, n_blocks), heads[i]],
                               nt, preferred_element_type=F32) * scale

    def finish(qi, which):
        for i in which:
            y = _rms(acc_ref[i], gn_ref[:, heads[i]])
            gate = g_ref[tile_rows(qi), heads[i]].astype(F32)
            o_ref[tile_rows(qi), heads[i]] = (y * _silu(gate)).astype(o_ref.dtype)

    def prefetch_next_tile(i, qi):
        nxt = jnp.minimum(qi + 1, SB_NQ - 1)
        zd_ref[i] = logits(i, nxt, nxt * SB_QB, SB_QB)
        z1_ref[i] = logits(i, nxt, nxt * SB_QB - 1, 1)

    def tile(qi, c):
        worst = []
        for i in range(SB_HG):
            zd, z1 = zd_ref[i], z1_ref[i]
            prefetch_next_tile(i, qi)
            pv, carry = _sb_step(zd, v_ref[tile_rows(qi), heads[i]], tt_ref[...], None, causal)
            a, carry = _sb_block(z1, tt_ref[...], carry)
            acc_ref[i] = pv + jnp.dot(a.astype(BF16), v_ref[key_rows(qi * SB_QB - 1, 1), heads[i]],
                                      preferred_element_type=F32)
            carry_ref[i] = carry
            worst.append(jnp.max(carry))

        def more(state):
            kb, *left = state
            return jnp.logical_and(kb >= 0, functools.reduce(jnp.maximum, left) > SB_DONE)

        def block_step(state):
            kb, *left = state

            def head_step(i):
                a, carry = _sb_block(logits(i, qi, kb, 1), tt_ref[...], carry_ref[i])
                acc_ref[i] = acc_ref[i] + jnp.dot(a.astype(BF16), v_ref[key_rows(kb, 1), heads[i]],
                                                  preferred_element_type=F32)
                carry_ref[i] = carry
                return jnp.max(carry)

            return (kb - 1, *[lax.cond(left[i] > SB_DONE, functools.partial(head_step, i), lambda i=i: left[i])
                              for i in range(SB_HG)])

        finish(qi, range(SB_HG))
        lax.while_loop(more, block_step, (qi * SB_QB - 2, *worst))
        for i in range(SB_HG):
            @pl.when(worst[i] > SB_DONE)
            def _(i=i):
                finish(qi, [i])

        return c

    for i in range(SB_HG):
        pv, _ = _sb_step(logits(i, 0, 0, SB_QB), v_ref[0:SB_TQ, heads[i]], tt_ref[...], None, causal)
        acc_ref[i] = pv
        prefetch_next_tile(i, 0)
    finish(0, range(SB_HG))
    lax.fori_loop(1, SB_NQ, tile, 0)


def _sb_attention(proj, tt, out_norm3, layer):
    group_cols = lambda first: pl.BlockSpec((SEQ, SB_GW), lambda b, h: (b, first // SB_GW + h))
    return pl.pallas_call(
        _sb_kernel,
        grid=(BATCH, SB_HEADS // SB_HG),
        in_specs=[group_cols(COL_SB_Q), group_cols(COL_SB_K), group_cols(COL_SB_V), group_cols(COL_SB_G),
                  pl.BlockSpec((2 * SB_TK, 2 * SB_TK), lambda b, h: (0, 0)),
                  pl.BlockSpec((None, 1, SB_GW), lambda b, h: (layer, 0, h))],
        out_specs=group_cols(0),
        out_shape=jax.ShapeDtypeStruct((TOKENS, SB_WIDTH), BF16),
        scratch_shapes=[pltpu.VMEM((SB_HG, SB_TQ, HEAD_DIM), F32), pltpu.VMEM((SB_HG, SB_TQ, SB_TK), F32),
                        pltpu.VMEM((SB_HG, SB_TQ, SB_TQ), F32), pltpu.VMEM((SB_HG, SB_TQ, SB_TK), F32)],
        compiler_params=_params(("parallel", "parallel")),
        name="sb_attention",
    )(proj, proj, proj, proj, tt, out_norm3)


GLA_RB = 256
GLA_NCH = GLA_RB // GLA_CHUNK
GLA_OFF_ROWS = GLA_HEADS * GLA_NCH * GLA_SUB
GLA_PAIR = 2 * GLA_HEAD_K


def _gla_kernel(q_ref, k_ref, v_ref, r_ref, g_ref, wup_ref, bup_ref, tril_ref, e_ref, bm_ref, om_ref, gn_ref,
                o_ref, st_ref):
    @pl.when(pl.program_id(1) == 0)
    def _():
        st_ref[...] = jnp.zeros_like(st_ref)

    R, C, S = GLA_RB, GLA_CHUNK, GLA_SUB
    chunks = [slice(c * C, (c + 1) * C) for c in range(GLA_NCH)]
    lane_k = lax.broadcasted_iota(jnp.int32, (1, GLA_KEY_WIDTH), 1)
    head_masks = [(lane_k // GLA_HEAD_K == h).astype(F32) for h in range(GLA_HEADS)]
    nt = (((1,), (1,)), ((), ()))
    tn = (((0,), (0,)), ((), ()))

    q = q_ref[...].astype(F32) * (GLA_HEAD_K ** -0.5)
    k = k_ref[...].astype(F32)
    v = v_ref[...]
    logits = jnp.dot(r_ref[...], wup_ref[...], preferred_element_type=F32) + bup_ref[...]
    log_alpha = (jnp.minimum(logits, 0.0) - jnp.log(1.0 + jnp.exp(-jnp.abs(logits)))) * (1.0 / GLA_GATE_TAU)
    hi, lo = _split_bf16(log_alpha)
    tril = tril_ref[...]
    bc = (jnp.dot(tril, hi, preferred_element_type=F32)
          + jnp.dot(tril, lo, preferred_element_type=F32))
    b_last = [bc[ch.stop - 1:ch.stop, :] for ch in chunks]

    q_dec = (q * jnp.exp(bc)).astype(BF16)
    k_dec = jnp.concatenate([k[ch] * jnp.exp(b_last[c] - bc[ch]) for c, ch in enumerate(chunks)],
                            axis=0).astype(BF16)
    o_inter = [[None] * (GLA_HEADS // 2) for _ in chunks]
    states = [st_ref[p] for p in range(GLA_HEADS // 2)]

    def recur_step(p, c):
        ch = chunks[c]
        ks = slice(p * GLA_PAIR, (p + 1) * GLA_PAIR)
        vs = slice(p * 2 * HEAD_DIM, (p + 1) * 2 * HEAD_DIM)
        st = states[p]
        o_inter[c][p] = lax.dot_general(q_dec[ch, ks], st.astype(BF16), nt, preferred_element_type=F32)
        upd = lax.dot_general(v[ch, vs], k_dec[ch, ks], tn, preferred_element_type=F32)
        states[p] = st * jnp.exp(b_last[c][:, ks]) + upd * bm_ref[...]

    off = []

    def off_step(i):
        q_parts, k_parts = [], []
        for ch in chunks:
            lo_r = ch.start + i * S
            b_row = bc[lo_r - 1:lo_r, :]
            q_parts.append(q[lo_r:lo_r + S, :] * jnp.exp(bc[lo_r:lo_r + S, :] - b_row))
            k_parts.append(k[ch] * jnp.exp(jnp.minimum(b_row - bc[ch], 0.0)))
        q_i = jnp.concatenate(q_parts, axis=0)
        k_i = jnp.concatenate(k_parts, axis=0).astype(BF16)
        lhs = jnp.concatenate([q_i * head_masks[h] for h in range(GLA_HEADS)], axis=0).astype(BF16)
        off.append(lax.dot_general(lhs, k_i, nt, preferred_element_type=F32) * om_ref[i - 1])

    def shifted(x, d):
        return pltpu.roll(x.reshape(R // S, S, GLA_KEY_WIDTH), d, 1).reshape(R, GLA_KEY_WIDTH)

    tmod = lax.broadcasted_iota(jnp.int32, (R, GLA_KEY_WIDTH), 0) % S
    diag = [(q * k).astype(BF16)]

    def diag_step(d):
        prod = q * shifted(k, d) * jnp.exp(bc - shifted(bc, d))
        diag.append(jnp.where(tmod >= d, prod, 0.0).astype(BF16))

    vector_work = ([functools.partial(off_step, i) for i in range(1, C // S)]
                   + [functools.partial(diag_step, d) for d in range(1, S)])
    chain = [functools.partial(recur_step, p, c) for c in range(GLA_NCH) for p in range(GLA_HEADS // 2)]
    emitted = 0
    for j, work in enumerate(vector_work):
        while emitted < len(chain) and emitted * len(vector_work) <= j * len(chain):
            chain[emitted]()
            emitted += 1
        work()
    for step in chain[emitted:]:
        step()
    for p in range(GLA_HEADS // 2):
        st_ref[p] = states[p]
    o = jnp.concatenate([jnp.concatenate(parts, axis=1) for parts in o_inter], axis=0)
    score = jnp.dot(jnp.concatenate(diag, axis=1), e_ref[...], preferred_element_type=F32)

    lane_head = lax.broadcasted_iota(jnp.int32, (C, LANES), 1) // S
    no_diag = jnp.zeros((C, LANES), F32)
    o_intra = []
    for h in range(GLA_HEADS):
        blocks = []
        for c, ch in enumerate(chunks):
            a_diag = pltpu.roll(jnp.where(lane_head == h, score[ch], 0.0),
                                (LANES - (S - 1) - S * h + C * c) % LANES, 1, stride=1, stride_axis=0)
            halves = [no_diag] * (R // LANES)
            halves[C * c // LANES] = a_diag
            first = (h * GLA_NCH + c) * S
            a_off = jnp.concatenate([jnp.zeros((S, R), F32)] + [a[first:first + S, :] for a in off], axis=0)
            blocks.append(a_off + jnp.concatenate(halves, axis=1))
        a_h = jnp.concatenate(blocks, axis=0).astype(BF16)
        o_intra.append(jnp.dot(a_h, v[:, h * HEAD_DIM:(h + 1) * HEAD_DIM], preferred_element_type=F32))
    o = o + jnp.concatenate(o_intra, axis=1)

    g = g_ref[...].astype(F32)
    for h in range(GLA_HEADS):
        hs = slice(h * HEAD_DIM, (h + 1) * HEAD_DIM)
        y = _rms(o[:, hs], gn_ref[:, hs])
        o_ref[:, hs] = (y * _silu(g[:, hs])).astype(o_ref.dtype)


def _gla(proj, proj_b, wup_p, bup3, tril, e_mat, bmask, offmask, out_norm3, layer):
    nr = SEQ // GLA_RB
    row = lambda b, r: b * nr + r
    return pl.pallas_call(
        _gla_kernel,
        grid=(BATCH, nr),
        in_specs=[pl.BlockSpec((GLA_RB, GLA_KEY_WIDTH), lambda b, r: (row(b, r), COL_GLA_Q // GLA_KEY_WIDTH)),
                  pl.BlockSpec((GLA_RB, GLA_KEY_WIDTH), lambda b, r: (row(b, r), COL_GLA_K // GLA_KEY_WIDTH)),
                  pl.BlockSpec((GLA_RB, GLA_WIDTH), lambda b, r: (row(b, r), COL_GLA_V // GLA_WIDTH)),
                  pl.BlockSpec((GLA_RB, LANES), lambda b, r: (row(b, r), COLB_GLA_R // LANES)),
                  pl.BlockSpec((GLA_RB, GLA_WIDTH), lambda b, r: (row(b, r), COLB_GLA_G // GLA_WIDTH)),
                  pl.BlockSpec((None, LANES, GLA_KEY_WIDTH), lambda b, r: (layer, 0, 0)),
                  pl.BlockSpec((None, 1, GLA_KEY_WIDTH), lambda b, r: (layer, 0, 0)),
                  pl.BlockSpec((GLA_RB, GLA_RB), lambda b, r: (0, 0)),
                  pl.BlockSpec((GLA_SUB * GLA_KEY_WIDTH, LANES), lambda b, r: (0, 0)),
                  pl.BlockSpec((2 * HEAD_DIM, GLA_PAIR), lambda b, r: (0, 0)),
                  pl.BlockSpec((GLA_CHUNK // GLA_SUB - 1, GLA_OFF_ROWS, GLA_RB), lambda b, r: (0, 0, 0)),
                  pl.BlockSpec((None, 1, GLA_WIDTH), lambda b, r: (layer, 0, SB_WIDTH // GLA_WIDTH))],
        out_specs=pl.BlockSpec((GLA_RB, GLA_WIDTH), lambda b, r: (row(b, r), 0)),
        out_shape=jax.ShapeDtypeStruct((TOKENS, GLA_WIDTH), BF16),
        scratch_shapes=[pltpu.VMEM((GLA_HEADS // 2, 2 * HEAD_DIM, GLA_PAIR), F32)],
        compiler_params=_params(("parallel", "arbitrary")),
        name="gla",
    )(proj, proj, proj, proj_b, proj_b, wup_p, bup3, tril, e_mat, bmask, offmask, out_norm3)


SGU_RB = 512


def _sgu_kernel(u_ref, v_ref, g_ref, w_ref, bb_ref, sn_ref, gn_ref, o_ref):
    C = SGU_CHUNK
    keep = (lax.broadcasted_iota(jnp.int32, (C, C), 0) >= lax.broadcasted_iota(jnp.int32, (C, C), 1))
    w = [jnp.where(keep, w_ref[g], 0.0).astype(BF16) for g in range(SGU_GROUPS)]
    for c in range(SGU_RB // C):
        rows = slice(c * C, (c + 1) * C)
        u = _gelu_tanh(u_ref[rows, :].astype(F32))
        v = _gelu_tanh(v_ref[rows, :].astype(F32))
        vn = _rms(v, sn_ref[...]).astype(BF16)
        gate = g_ref[rows, :].astype(F32)
        for g in range(SGU_GROUPS):
            gs = slice(g * HEAD_DIM, (g + 1) * HEAD_DIM)
            mixed = jnp.dot(w[g], vn[:, gs], preferred_element_type=F32) + bb_ref[g]
            y = _rms(u[:, gs] * mixed, gn_ref[:, gs])
            o_ref[rows, gs] = (y * _silu(gate[:, gs])).astype(o_ref.dtype)


def _sgu(proj, w_sgu, b_sgu_b, sgu_norm3, out_norm3, layer):
    return pl.pallas_call(
        _sgu_kernel,
        grid=(TOKENS // SGU_RB,),
        in_specs=[pl.BlockSpec((SGU_RB, SGU_WIDTH), lambda i: (i, COLB_SGU_U // SGU_WIDTH)),
                  pl.BlockSpec((SGU_RB, SGU_WIDTH), lambda i: (i, COLB_SGU_V // SGU_WIDTH)),
                  pl.BlockSpec((SGU_RB, SGU_WIDTH), lambda i: (i, COLB_SGU_G // SGU_WIDTH)),
                  pl.BlockSpec((None, SGU_GROUPS, SGU_CHUNK, SGU_CHUNK), lambda i: (layer, 0, 0, 0)),
                  pl.BlockSpec((None, SGU_GROUPS, SGU_CHUNK, HEAD_DIM), lambda i: (layer, 0, 0, 0)),
                  pl.BlockSpec((None, 1, SGU_WIDTH), lambda i: (layer, 0, 0)),
                  pl.BlockSpec((None, 1, SGU_WIDTH), lambda i: (layer, 0, (SB_WIDTH + GLA_WIDTH) // SGU_WIDTH))],
        out_specs=pl.BlockSpec((SGU_RB, SGU_WIDTH), lambda i: (i, 0)),
        out_shape=jax.ShapeDtypeStruct((TOKENS, SGU_WIDTH), BF16),
        compiler_params=_params(("parallel",)),
        name="sgu",
    )(proj, proj, proj, w_sgu, b_sgu_b, sgu_norm3, out_norm3)


POST_TM = 512


def _post_kernel(x_ref, msb_ref, mgla_ref, msgu_ref, wout_ref, nx_ref, wxq_ref, kv_ref, wxo_ref, nn_ref,
                 *out_refs, last):
    mix = jnp.concatenate([msb_ref[...], mgla_ref[...], msgu_ref[...]], axis=1)
    x1 = x_ref[...] + jnp.dot(mix, wout_ref[...], preferred_element_type=F32)

    hx = _rms(x1, nx_ref[...]).astype(BF16)
    q = jnp.dot(hx, wxq_ref[...], preferred_element_type=F32).astype(BF16)
    scale = HEAD_DIM ** -0.5
    heads = []
    for h in range(XA_HEADS):
        hs = slice(h * HEAD_DIM, (h + 1) * HEAD_DIM)
        k_h = kv_ref[:, hs]
        v_h = kv_ref[:, XA_WIDTH + h * HEAD_DIM:XA_WIDTH + (h + 1) * HEAD_DIM]
        s = lax.dot_general(q[:, hs], k_h, (((1,), (1,)), ((), ())), preferred_element_type=F32) * scale
        e = jnp.exp(s - jnp.max(s, axis=-1, keepdims=True))
        p = e * (1.0 / jnp.sum(e, axis=-1, keepdims=True))
        heads.append(jnp.dot(p.astype(BF16), v_h, preferred_element_type=F32).astype(BF16))
    x2 = x1 + jnp.dot(jnp.concatenate(heads, axis=1), wxo_ref[...], preferred_element_type=F32)

    if last:
        out_refs[0][...] = _rms(x2, nn_ref[...])
    else:
        out_refs[0][...] = x2
        out_refs[1][...] = _rms(x2, nn_ref[...]).astype(BF16)


def _post(x, m_sb, m_gla, m_sgu, w_out_bf, norm_x3, w_xq_bf, kv, w_xo_bf, next_norm3, layer, next_idx, last):
    tm = POST_TM
    per_b = SEQ // tm
    row_spec = lambda width: pl.BlockSpec((tm, width), lambda i: (i, 0))
    const = lambda *shape: pl.BlockSpec((None,) + shape, lambda i: (layer,) + (0,) * len(shape),
                                        pipeline_mode=pl.Buffered(1))
    x_out = jax.ShapeDtypeStruct((TOKENS, D_MODEL), F32)
    if last:
        out_shape, out_specs = x_out, row_spec(D_MODEL)
    else:
        out_shape = (x_out, jax.ShapeDtypeStruct((TOKENS, D_MODEL), BF16))
        out_specs = (row_spec(D_MODEL), row_spec(D_MODEL))
    return pl.pallas_call(
        functools.partial(_post_kernel, last=last),
        grid=(TOKENS // tm,),
        in_specs=[row_spec(D_MODEL), row_spec(SB_WIDTH), row_spec(GLA_WIDTH), row_spec(SGU_WIDTH),
                  const(D_MODEL, D_MODEL),
                  const(1, D_MODEL),
                  const(D_MODEL, XA_WIDTH),
                  pl.BlockSpec((None, N_MEM, 2 * XA_WIDTH), lambda i: (layer, i // per_b, 0)),
                  const(XA_WIDTH, D_MODEL),
                  pl.BlockSpec((None, 1, D_MODEL), lambda i: (next_idx, 0, 0))],
        out_specs=out_specs,
        out_shape=out_shape,
        compiler_params=_params(("parallel",)),
        name="post",
    )(x, m_sb, m_gla, m_sgu, w_out_bf, norm_x3, w_xq_bf, kv, w_xo_bf, next_norm3)


def _constants():
    r = jnp.arange(2 * SB_TK)[:, None]
    c = jnp.arange(2 * SB_TK)[None, :]
    tt = jnp.where(c < SB_TK, (r % SB_TK) > c, True).astype(BF16)
    t = jnp.arange(GLA_RB)
    tril = ((t[:, None] >= t[None, :]) & (t[:, None] // GLA_CHUNK == t[None, :] // GLA_CHUNK)).astype(BF16)
    row_chunk = (jnp.arange(GLA_OFF_ROWS)[:, None] // GLA_SUB) % GLA_NCH
    offmask = jnp.stack([(row_chunk == t[None, :] // GLA_CHUNK) & (t[None, :] % GLA_CHUNK < i * GLA_SUB)
                         for i in range(1, GLA_CHUNK // GLA_SUB)]).astype(F32)
    e_row = jnp.arange(GLA_SUB * GLA_KEY_WIDTH)
    e_lane = GLA_SUB * ((e_row % GLA_KEY_WIDTH) // GLA_HEAD_K) + (GLA_SUB - 1) - e_row // GLA_KEY_WIDTH
    e_mat = (e_lane[:, None] == jnp.arange(LANES)[None, :]).astype(BF16)
    bmask = (jnp.arange(2 * HEAD_DIM)[:, None] // HEAD_DIM
             == jnp.arange(GLA_PAIR)[None, :] // GLA_HEAD_K).astype(F32)
    return tt, tril, e_mat, bmask, offmask


def kernel(x, mem, norm_mix, w_in, w_gla_gate_up, b_gla_gate, sgu_norm, w_sgu, b_sgu, out_norm, w_out,
           norm_xattn, norm_mem, w_xq, w_xkv, w_xo, final_norm):
    tt, tril, e_mat, bmask, offmask = _constants()
    w_in_t = jnp.swapaxes(w_in, 1, 2)
    wup_p = jnp.pad(w_gla_gate_up, ((0, 0), (0, LANES - GLA_GATE_RANK), (0, 0))).astype(BF16)
    w_out_bf, w_xq_bf, w_xo_bf = (w.astype(BF16) for w in (w_out, w_xq, w_xo))
    row3 = lambda a: a.reshape(a.shape[0], 1, a.shape[-1])
    norm_mix3, bup3, sgu_norm3, out_norm3 = row3(norm_mix), row3(b_gla_gate), row3(sgu_norm), row3(out_norm)
    norm_x3, norm_mem3 = row3(norm_xattn), row3(norm_mem)
    final3 = final_norm.reshape(1, 1, D_MODEL)
    b_sgu_b = jnp.broadcast_to(b_sgu[..., None], b_sgu.shape + (HEAD_DIM,))

    xf = x.reshape(TOKENS, D_MODEL)
    kv = _mem_kv(mem.reshape(BATCH * N_MEM, D_MODEL), norm_mem3, w_xkv)
    h = _norm_rows(xf, norm_mix3, 0)
    for l in range(DEPTH):
        last = l == DEPTH - 1
        proj = _in_proj_a(h, w_in_t, l)
        proj_b = _in_proj_b(h, w_in_t, l)
        m_sb = _sb_attention(proj, tt, out_norm3, l)
        m_gla = _gla(proj, proj_b, wup_p, bup3, tril, e_mat, bmask, offmask, out_norm3, l)
        m_sgu = _sgu(proj_b, w_sgu, b_sgu_b, sgu_norm3, out_norm3, l)
        nxt = final3 if last else norm_mix3
        res = _post(xf, m_sb, m_gla, m_sgu, w_out_bf, norm_x3, w_xq_bf, kv, w_xo_bf, nxt,
                    l, 0 if last else l + 1, last)
        if last:
            xf = res
        else:
            xf, h = res
    return xf.reshape(BATCH, SEQ, D_MODEL)
```

```python
import functools

import jax
import jax.numpy as jnp
from jax import lax
from jax.experimental import pallas as pl
from jax.experimental.pallas import tpu as pltpu

F32 = jnp.float32
BF16 = jnp.bfloat16

D_MODEL = 2048
BATCH = 4
SEQ = 2048
DEPTH = 4
TOKENS = BATCH * SEQ
HEAD_DIM = 128
SB_HEADS = 8
SB_WIDTH = SB_HEADS * HEAD_DIM
GLA_HEADS = 4
GLA_HEAD_K = 64
GLA_WIDTH = GLA_HEADS * HEAD_DIM
GLA_KEY_WIDTH = GLA_HEADS * GLA_HEAD_K
GLA_GATE_RANK = 16
GLA_GATE_TAU = 16.0
GLA_CHUNK = 64
GLA_SUB = 8
SGU_GROUPS = 4
SGU_WIDTH = 512
SGU_CHUNK = 128
N_MEM = 256
XA_HEADS = 4
XA_WIDTH = XA_HEADS * HEAD_DIM
EPS = 1e-6

LANES = 128

COL_SB_Q = 0
COL_SB_K = SB_WIDTH
COL_SB_V = 2 * SB_WIDTH
COL_SB_G = 3 * SB_WIDTH
COL_GLA_Q = 4 * SB_WIDTH
COL_GLA_K = COL_GLA_Q + GLA_KEY_WIDTH
COL_GLA_V = COL_GLA_K + GLA_KEY_WIDTH
PROJ_A_WIDTH = COL_GLA_V + GLA_WIDTH
COLB_GLA_G = 0
COLB_SGU_U = GLA_WIDTH
COLB_SGU_V = COLB_SGU_U + SGU_WIDTH
COLB_SGU_G = COLB_SGU_V + SGU_WIDTH
COLB_GLA_R = COLB_SGU_G + SGU_WIDTH
PROJ_B_MAIN = COLB_GLA_R
PROJ_B_WIDTH = COLB_GLA_R + LANES
ORIG_GLA_R = PROJ_A_WIDTH

VMEM_LIMIT = 56 * 1024 * 1024


def _params(semantics):
    return pltpu.CompilerParams(dimension_semantics=semantics, vmem_limit_bytes=VMEM_LIMIT)


def _rms(x, g):
    ms = jnp.mean(x * x, axis=-1, keepdims=True)
    return x * lax.rsqrt(ms + EPS) * g


def _silu(g):
    half = 0.5 * g
    return half + half * jnp.tanh(half)


def _softplus(z):
    return jnp.maximum(z, 0.0) + jnp.log(1.0 + jnp.exp(-jnp.abs(z)))


def _split_bf16(x):
    hi = x.astype(BF16)
    lo = (x - hi.astype(F32)).astype(BF16)
    return hi, lo


def _gelu_tanh(x):
    c = 0.7978845608028654
    half = 0.5 * x
    return half + half * jnp.tanh(x * (c * 0.044715 * (x * x) + c))


def _norm_kernel(x_ref, g_ref, o_ref):
    o_ref[...] = _rms(x_ref[...], g_ref[...]).astype(o_ref.dtype)


def _norm_rows(x2d, g3d, layer, tm=1024):
    rows = x2d.shape[0]
    return pl.pallas_call(
        _norm_kernel,
        grid=(rows // tm,),
        in_specs=[pl.BlockSpec((tm, D_MODEL), lambda i: (i, 0)),
                  pl.BlockSpec((None, 1, D_MODEL), lambda i: (layer, 0, 0))],
        out_specs=pl.BlockSpec((tm, D_MODEL), lambda i: (i, 0)),
        out_shape=jax.ShapeDtypeStruct((rows, D_MODEL), BF16),
        compiler_params=_params(("parallel",)),
        name="rmsnorm_rows",
    )(x2d, g3d)


def _memkv_kernel(m_ref, g_ref, w_ref, o_ref):
    mn = _rms(m_ref[...], g_ref[...]).astype(BF16)
    o_ref[...] = jnp.dot(mn, w_ref[...].astype(BF16), preferred_element_type=F32).astype(o_ref.dtype)


def _mem_kv(mem2d, norm_mem3, w_xkv):
    rows = mem2d.shape[0]
    return pl.pallas_call(
        _memkv_kernel,
        grid=(DEPTH,),
        in_specs=[pl.BlockSpec((rows, D_MODEL), lambda l: (0, 0)),
                  pl.BlockSpec((None, 1, D_MODEL), lambda l: (l, 0, 0)),
                  pl.BlockSpec((None, D_MODEL, 2 * XA_WIDTH), lambda l: (l, 0, 0))],
        out_specs=pl.BlockSpec((None, rows, 2 * XA_WIDTH), lambda l: (l, 0, 0)),
        out_shape=jax.ShapeDtypeStruct((DEPTH, rows, 2 * XA_WIDTH), BF16),
        compiler_params=_params(("parallel",)),
        name="mem_kv",
    )(mem2d, norm_mem3, w_xkv)


W_CONV_ROWS = 256


def _store_transposed_bf16(wt_ref, wbf_ref, n_rows):
    def conv(i, c):
        r0 = pl.multiple_of(i * W_CONV_ROWS, W_CONV_ROWS)
        wbf_ref[:, pl.ds(r0, W_CONV_ROWS)] = wt_ref[pl.ds(r0, W_CONV_ROWS), :].T.astype(BF16)
        return c

    lax.fori_loop(0, n_rows // W_CONV_ROWS, conv, 0)


def _proj_a_kernel(h_ref, wt_ref, o_ref, wbf_ref):
    @pl.when(pl.program_id(1) == 0)
    def _():
        _store_transposed_bf16(wt_ref, wbf_ref, wt_ref.shape[0])

    o_ref[...] = jnp.dot(h_ref[...], wbf_ref[...], preferred_element_type=F32).astype(o_ref.dtype)


def _in_proj_a(h, w_in_t, layer, tm=1024, tn=1280):
    return pl.pallas_call(
        _proj_a_kernel,
        grid=(PROJ_A_WIDTH // tn, TOKENS // tm),
        in_specs=[pl.BlockSpec((tm, D_MODEL), lambda n, m: (m, 0)),
                  pl.BlockSpec((None, tn, D_MODEL), lambda n, m: (layer, n, 0))],
        out_specs=pl.BlockSpec((tm, tn), lambda n, m: (m, n)),
        out_shape=jax.ShapeDtypeStruct((TOKENS, PROJ_A_WIDTH), BF16),
        scratch_shapes=[pltpu.VMEM((D_MODEL, tn), BF16)],
        compiler_params=_params(("parallel", "arbitrary")),
        name="in_proj_a",
    )(h, w_in_t)


def _proj_b_kernel(h_ref, wt_ref, wr_ref, o_ref, wbf_ref):
    @pl.when(pl.program_id(0) == 0)
    def _():
        _store_transposed_bf16(wt_ref, wbf_ref, PROJ_B_MAIN)
        rank = jnp.concatenate([wr_ref[...], jnp.zeros((LANES - GLA_GATE_RANK, D_MODEL), F32)], axis=0)
        wbf_ref[:, PROJ_B_MAIN:] = rank.T.astype(BF16)

    o_ref[...] = jnp.dot(h_ref[...], wbf_ref[...], preferred_element_type=F32).astype(o_ref.dtype)


def _in_proj_b(h, w_in_t, layer, tm=1024):
    once = pl.Buffered(1)
    window = lambda rows, start: pl.BlockSpec((None, pl.Element(rows), pl.Element(D_MODEL)),
                                              lambda m: (layer, start, 0), pipeline_mode=once)
    return pl.pallas_call(
        _proj_b_kernel,
        grid=(TOKENS // tm,),
        in_specs=[pl.BlockSpec((tm, D_MODEL), lambda m: (m, 0)),
                  window(PROJ_B_MAIN, ORIG_GLA_R + GLA_GATE_RANK),
                  window(GLA_GATE_RANK, ORIG_GLA_R)],
        out_specs=pl.BlockSpec((tm, PROJ_B_WIDTH), lambda m: (m, 0)),
        out_shape=jax.ShapeDtypeStruct((TOKENS, PROJ_B_WIDTH), BF16),
        scratch_shapes=[pltpu.VMEM((D_MODEL, PROJ_B_WIDTH), BF16)],
        compiler_params=_params(("arbitrary",)),
        name="in_proj_b",
    )(h, w_in_t, w_in_t)


SB_TQ = 256
SB_TK = 128
SB_NQ = SEQ // SB_TQ
SB_QB = SB_TQ // SB_TK
SB_DONE = -88.0
SB_HG = 4
SB_GW = SB_HG * HEAD_DIM


def _sb_suffix(log_keep, tt):
    hi, lo = _split_bf16(log_keep)
    return jnp.dot(jnp.concatenate([hi, lo], axis=1), tt, preferred_element_type=F32)


def _sb_own_rows(z, v, tt, causal):
    sp = _softplus(z)
    log_keep = jnp.where(causal, -sp, 0.0)
    cs_r = _sb_suffix(log_keep[:, SB_TK:], tt)
    cs_l = _sb_suffix(log_keep[:, :SB_TK], tt)
    log_beta = z - sp
    carry_l = cs_r[:, SB_TK:]
    a = jnp.exp(jnp.concatenate([log_beta[:, :SB_TK] + cs_l[:, :SB_TK] + carry_l,
                                 log_beta[:, SB_TK:] + cs_r[:, :SB_TK]], axis=1))
    a = jnp.where(causal, a, 0.0)
    return jnp.dot(a.astype(BF16), v, preferred_element_type=F32), carry_l + cs_l[:, SB_TK:]


def _sb_block(z, tt, carry):
    sp = _softplus(z)
    cs = _sb_suffix(-sp, tt)
    return jnp.exp((z - sp) + cs[:, :SB_TK] + carry), carry + cs[:, SB_TK:]


def _sb_kernel(q_ref, k_ref, v_ref, g_ref, tt_ref, gn_ref, o_ref, acc_ref, carry_ref, zd_ref, z1_ref):
    scale = HEAD_DIM ** -0.5
    nt = (((1,), (1,)), ((), ()))
    heads = [slice(i * HEAD_DIM, (i + 1) * HEAD_DIM) for i in range(SB_HG)]
    causal = (lax.broadcasted_iota(jnp.int32, (SB_TQ, SB_TQ), 1)
              < lax.broadcasted_iota(jnp.int32, (SB_TQ, SB_TQ), 0))

    def key_rows(first_block, n_blocks):
        if isinstance(first_block, int):
            return slice(first_block * SB_TK, (first_block + n_blocks) * SB_TK)
        return pl.ds(pl.multiple_of(first_block * SB_TK, SB_TK), n_blocks * SB_TK)

    def tile_rows(t):
        return key_rows(t * SB_QB, SB_QB)

    def logits(i, qt, first_block, n_blocks):
        return lax.dot_general(q_ref[tile_rows(qt), heads[i]], k_ref[key_rows(first_block, n_blocks), heads[i]],
                               nt, preferred_element_type=F32) * scale

    def finish(qi, which):
        for i in which:
            y = _rms(acc_ref[i], gn_ref[:, heads[i]])
            gate = g_ref[tile_rows(qi), heads[i]].astype(F32)
            o_ref[tile_rows(qi), heads[i]] = (y * _silu(gate)).astype(o_ref.dtype)

    def prefetch_next_tile(i, qi):
        nxt = jnp.minimum(qi + 1, SB_NQ - 1)
        zd_ref[i] = logits(i, nxt, nxt * SB_QB, SB_QB)
        z1_ref[i] = logits(i, nxt, nxt * SB_QB - 1, 1)

    def tile(qi, c):
        worst = []
        for i in range(SB_HG):
            zd, z1 = zd_ref[i], z1_ref[i]
            prefetch_next_tile(i, qi)
            pv, carry = _sb_own_rows(zd, v_ref[tile_rows(qi), heads[i]], tt_ref[...], causal)
            a, carry = _sb_block(z1, tt_ref[...], carry)
            acc_ref[i] = pv + jnp.dot(a.astype(BF16), v_ref[key_rows(qi * SB_QB - 1, 1), heads[i]],
                                      preferred_element_type=F32)
            carry_ref[i] = carry
            worst.append(jnp.max(carry))

        def more(state):
            kb, *left = state
            return jnp.logical_and(kb >= 0, functools.reduce(jnp.maximum, left) > SB_DONE)

        def block_step(state):
            kb, *left = state

            def head_step(i):
                a, carry = _sb_block(logits(i, qi, kb, 1), tt_ref[...], carry_ref[i])
                acc_ref[i] = acc_ref[i] + jnp.dot(a.astype(BF16), v_ref[key_rows(kb, 1), heads[i]],
                                                  preferred_element_type=F32)
                carry_ref[i] = carry
                return jnp.max(carry)

            return (kb - 1, *[lax.cond(left[i] > SB_DONE, functools.partial(head_step, i), lambda i=i: left[i])
                              for i in range(SB_HG)])

        finish(qi, range(SB_HG))
        lax.while_loop(more, block_step, (qi * SB_QB - 2, *worst))
        for i in range(SB_HG):
            @pl.when(worst[i] > SB_DONE)
            def _(i=i):
                finish(qi, [i])

        return c

    for i in range(SB_HG):
        pv, _ = _sb_own_rows(logits(i, 0, 0, SB_QB), v_ref[0:SB_TQ, heads[i]], tt_ref[...], causal)
        acc_ref[i] = pv
        prefetch_next_tile(i, 0)
    finish(0, range(SB_HG))
    lax.fori_loop(1, SB_NQ, tile, 0)


def _sb_attention(proj, tt, out_norm3, layer):
    group_cols = lambda first: pl.BlockSpec((SEQ, SB_GW), lambda b, h: (b, first // SB_GW + h))
    return pl.pallas_call(
        _sb_kernel,
        grid=(BATCH, SB_HEADS // SB_HG),
        in_specs=[group_cols(COL_SB_Q), group_cols(COL_SB_K), group_cols(COL_SB_V), group_cols(COL_SB_G),
                  pl.BlockSpec((2 * SB_TK, 2 * SB_TK), lambda b, h: (0, 0)),
                  pl.BlockSpec((None, 1, SB_GW), lambda b, h: (layer, 0, h))],
        out_specs=group_cols(0),
        out_shape=jax.ShapeDtypeStruct((TOKENS, SB_WIDTH), BF16),
        scratch_shapes=[pltpu.VMEM((SB_HG, SB_TQ, HEAD_DIM), F32), pltpu.VMEM((SB_HG, SB_TQ, SB_TK), F32),
                        pltpu.VMEM((SB_HG, SB_TQ, SB_TQ), F32), pltpu.VMEM((SB_HG, SB_TQ, SB_TK), F32)],
        compiler_params=_params(("parallel", "parallel")),
        name="sb_attention",
    )(proj, proj, proj, proj, tt, out_norm3)


GLA_RB = 256
GLA_NCH = GLA_RB // GLA_CHUNK
GLA_OFF_ROWS = GLA_HEADS * GLA_NCH * GLA_SUB
GLA_PAIR = 2 * GLA_HEAD_K


def _gla_kernel(q_ref, k_ref, v_ref, r_ref, g_ref, wup_ref, bup_ref, tril_ref, e_ref, bm_ref, om_ref, gn_ref,
                o_ref, st_ref):
    @pl.when(pl.program_id(1) == 0)
    def _():
        st_ref[...] = jnp.zeros_like(st_ref)

    R, C, S = GLA_RB, GLA_CHUNK, GLA_SUB
    chunks = [slice(c * C, (c + 1) * C) for c in range(GLA_NCH)]
    lane_k = lax.broadcasted_iota(jnp.int32, (1, GLA_KEY_WIDTH), 1)
    head_masks = [(lane_k // GLA_HEAD_K == h).astype(F32) for h in range(GLA_HEADS)]
    nt = (((1,), (1,)), ((), ()))
    tn = (((0,), (0,)), ((), ()))

    q = q_ref[...].astype(F32) * (GLA_HEAD_K ** -0.5)
    k = k_ref[...].astype(F32)
    v = v_ref[...]
    logits = jnp.dot(r_ref[...], wup_ref[...], preferred_element_type=F32) + bup_ref[...]
    log_alpha = (jnp.minimum(logits, 0.0) - jnp.log(1.0 + jnp.exp(-jnp.abs(logits)))) * (1.0 / GLA_GATE_TAU)
    hi, lo = _split_bf16(log_alpha)
    tril = tril_ref[...]
    bc = (jnp.dot(tril, hi, preferred_element_type=F32)
          + jnp.dot(tril, lo, preferred_element_type=F32))
    b_last = [bc[ch.stop - 1:ch.stop, :] for ch in chunks]

    q_dec = (q * jnp.exp(bc)).astype(BF16)
    k_dec = jnp.concatenate([k[ch] * jnp.exp(b_last[c] - bc[ch]) for c, ch in enumerate(chunks)],
                            axis=0).astype(BF16)
    o_inter = [[None] * (GLA_HEADS // 2) for _ in chunks]
    states = [st_ref[p] for p in range(GLA_HEADS // 2)]

    def recur_step(p, c):
        ch = chunks[c]
        ks = slice(p * GLA_PAIR, (p + 1) * GLA_PAIR)
        vs = slice(p * 2 * HEAD_DIM, (p + 1) * 2 * HEAD_DIM)
        st = states[p]
        o_inter[c][p] = lax.dot_general(q_dec[ch, ks], st.astype(BF16), nt, preferred_element_type=F32)
        upd = lax.dot_general(v[ch, vs], k_dec[ch, ks], tn, preferred_element_type=F32)
        states[p] = st * jnp.exp(b_last[c][:, ks]) + upd * bm_ref[...]

    off = []

    def off_step(i):
        q_parts, k_parts = [], []
        for ch in chunks:
            lo_r = ch.start + i * S
            b_row = bc[lo_r - 1:lo_r, :]
            q_parts.append(q[lo_r:lo_r + S, :] * jnp.exp(bc[lo_r:lo_r + S, :] - b_row))
            k_parts.append(k[ch] * jnp.exp(jnp.minimum(b_row - bc[ch], 0.0)))
        q_i = jnp.concatenate(q_parts, axis=0)
        k_i = jnp.concatenate(k_parts, axis=0).astype(BF16)
        lhs = jnp.concatenate([q_i * head_masks[h] for h in range(GLA_HEADS)], axis=0).astype(BF16)
        off.append(lax.dot_general(lhs, k_i, nt, preferred_element_type=F32) * om_ref[i - 1])

    def shifted(x, d):
        return pltpu.roll(x.reshape(R // S, S, GLA_KEY_WIDTH), d, 1).reshape(R, GLA_KEY_WIDTH)

    tmod = lax.broadcasted_iota(jnp.int32, (R, GLA_KEY_WIDTH), 0) % S
    diag = [(q * k).astype(BF16)]

    def diag_step(d):
        prod = q * shifted(k, d) * jnp.exp(bc - shifted(bc, d))
        diag.append(jnp.where(tmod >= d, prod, 0.0).astype(BF16))

    vector_work = ([functools.partial(off_step, i) for i in range(1, C // S)]
                   + [functools.partial(diag_step, d) for d in range(1, S)])
    chain = [functools.partial(recur_step, p, c) for c in range(GLA_NCH) for p in range(GLA_HEADS // 2)]
    emitted = 0
    for j, work in enumerate(vector_work):
        while emitted < len(chain) and emitted * len(vector_work) <= j * len(chain):
            chain[emitted]()
            emitted += 1
        work()
    for step in chain[emitted:]:
        step()
    for p in range(GLA_HEADS // 2):
        st_ref[p] = states[p]
    o = jnp.concatenate([jnp.concatenate(parts, axis=1) for parts in o_inter], axis=0)
    score = jnp.dot(jnp.concatenate(diag, axis=1), e_ref[...], preferred_element_type=F32)

    lane_head = lax.broadcasted_iota(jnp.int32, (C, LANES), 1) // S
    no_diag = jnp.zeros((C, LANES), F32)
    o_intra = []
    for h in range(GLA_HEADS):
        blocks = []
        for c, ch in enumerate(chunks):
            a_diag = pltpu.roll(jnp.where(lane_head == h, score[ch], 0.0),
                                (LANES - (S - 1) - S * h + C * c) % LANES, 1, stride=1, stride_axis=0)
            halves = [no_diag] * (R // LANES)
            halves[C * c // LANES] = a_diag
            first = (h * GLA_NCH + c) * S
            a_off = jnp.concatenate([jnp.zeros((S, R), F32)] + [a[first:first + S, :] for a in off], axis=0)
            blocks.append(a_off + jnp.concatenate(halves, axis=1))
        a_h = jnp.concatenate(blocks, axis=0).astype(BF16)
        o_intra.append(jnp.dot(a_h, v[:, h * HEAD_DIM:(h + 1) * HEAD_DIM], preferred_element_type=F32))
    o = o + jnp.concatenate(o_intra, axis=1)

    g = g_ref[...].astype(F32)
    for h in range(GLA_HEADS):
        hs = slice(h * HEAD_DIM, (h + 1) * HEAD_DIM)
        y = _rms(o[:, hs], gn_ref[:, hs])
        o_ref[:, hs] = (y * _silu(g[:, hs])).astype(o_ref.dtype)


def _gla(proj, proj_b, wup_p, bup3, tril, e_mat, bmask, offmask, out_norm3, layer):
    nr = SEQ // GLA_RB
    row = lambda b, r: b * nr + r
    return pl.pallas_call(
        _gla_kernel,
        grid=(BATCH, nr),
        in_specs=[pl.BlockSpec((GLA_RB, GLA_KEY_WIDTH), lambda b, r: (row(b, r), COL_GLA_Q // GLA_KEY_WIDTH)),
                  pl.BlockSpec((GLA_RB, GLA_KEY_WIDTH), lambda b, r: (row(b, r), COL_GLA_K // GLA_KEY_WIDTH)),
                  pl.BlockSpec((GLA_RB, GLA_WIDTH), lambda b, r: (row(b, r), COL_GLA_V // GLA_WIDTH)),
                  pl.BlockSpec((GLA_RB, LANES), lambda b, r: (row(b, r), COLB_GLA_R // LANES)),
                  pl.BlockSpec((GLA_RB, GLA_WIDTH), lambda b, r: (row(b, r), COLB_GLA_G // GLA_WIDTH)),
                  pl.BlockSpec((None, LANES, GLA_KEY_WIDTH), lambda b, r: (layer, 0, 0)),
                  pl.BlockSpec((None, 1, GLA_KEY_WIDTH), lambda b, r: (layer, 0, 0)),
                  pl.BlockSpec((GLA_RB, GLA_RB), lambda b, r: (0, 0)),
                  pl.BlockSpec((GLA_SUB * GLA_KEY_WIDTH, LANES), lambda b, r: (0, 0)),
                  pl.BlockSpec((2 * HEAD_DIM, GLA_PAIR), lambda b, r: (0, 0)),
                  pl.BlockSpec((GLA_CHUNK // GLA_SUB - 1, GLA_OFF_ROWS, GLA_RB), lambda b, r: (0, 0, 0)),
                  pl.BlockSpec((None, 1, GLA_WIDTH), lambda b, r: (layer, 0, SB_WIDTH // GLA_WIDTH))],
        out_specs=pl.BlockSpec((GLA_RB, GLA_WIDTH), lambda b, r: (row(b, r), 0)),
        out_shape=jax.ShapeDtypeStruct((TOKENS, GLA_WIDTH), BF16),
        scratch_shapes=[pltpu.VMEM((GLA_HEADS // 2, 2 * HEAD_DIM, GLA_PAIR), F32)],
        compiler_params=_params(("parallel", "arbitrary")),
        name="gla",
    )(proj, proj, proj, proj_b, proj_b, wup_p, bup3, tril, e_mat, bmask, offmask, out_norm3)


SGU_RB = 1024


def _sgu_kernel(u_ref, v_ref, g_ref, w_ref, bb_ref, sn_ref, gn_ref, o_ref):
    C = SGU_CHUNK
    keep = (lax.broadcasted_iota(jnp.int32, (C, C), 0) >= lax.broadcasted_iota(jnp.int32, (C, C), 1))
    w = [jnp.where(keep, w_ref[g], 0.0).astype(BF16) for g in range(SGU_GROUPS)]
    for c in range(SGU_RB // C):
        rows = slice(c * C, (c + 1) * C)
        u = _gelu_tanh(u_ref[rows, :].astype(F32))
        v = _gelu_tanh(v_ref[rows, :].astype(F32))
        vn = _rms(v, sn_ref[...]).astype(BF16)
        gate = g_ref[rows, :].astype(F32)
        for g in range(SGU_GROUPS):
            gs = slice(g * HEAD_DIM, (g + 1) * HEAD_DIM)
            mixed = jnp.dot(w[g], vn[:, gs], preferred_element_type=F32) + bb_ref[g]
            y = _rms(u[:, gs] * mixed, gn_ref[:, gs])
            o_ref[rows, gs] = (y * _silu(gate[:, gs])).astype(o_ref.dtype)


def _sgu(proj, w_sgu, b_sgu_b, sgu_norm3, out_norm3, layer):
    return pl.pallas_call(
        _sgu_kernel,
        grid=(TOKENS // SGU_RB,),
        in_specs=[pl.BlockSpec((SGU_RB, SGU_WIDTH), lambda i: (i, COLB_SGU_U // SGU_WIDTH)),
                  pl.BlockSpec((SGU_RB, SGU_WIDTH), lambda i: (i, COLB_SGU_V // SGU_WIDTH)),
                  pl.BlockSpec((SGU_RB, SGU_WIDTH), lambda i: (i, COLB_SGU_G // SGU_WIDTH)),
                  pl.BlockSpec((None, SGU_GROUPS, SGU_CHUNK, SGU_CHUNK), lambda i: (layer, 0, 0, 0)),
                  pl.BlockSpec((None, SGU_GROUPS, SGU_CHUNK, HEAD_DIM), lambda i: (layer, 0, 0, 0)),
                  pl.BlockSpec((None, 1, SGU_WIDTH), lambda i: (layer, 0, 0)),
                  pl.BlockSpec((None, 1, SGU_WIDTH), lambda i: (layer, 0, (SB_WIDTH + GLA_WIDTH) // SGU_WIDTH))],
        out_specs=pl.BlockSpec((SGU_RB, SGU_WIDTH), lambda i: (i, 0)),
        out_shape=jax.ShapeDtypeStruct((TOKENS, SGU_WIDTH), BF16),
        compiler_params=_params(("parallel",)),
        name="sgu",
    )(proj, proj, proj, w_sgu, b_sgu_b, sgu_norm3, out_norm3)


POST_TM = 512


def _post_kernel(x_ref, msb_ref, mgla_ref, msgu_ref, wout_ref, nx_ref, wxq_ref, kv_ref, wxo_ref, nn_ref,
                 *out_refs, last):
    mix = jnp.concatenate([msb_ref[...], mgla_ref[...], msgu_ref[...]], axis=1)
    x1 = x_ref[...] + jnp.dot(mix, wout_ref[...], preferred_element_type=F32)

    hx = _rms(x1, nx_ref[...]).astype(BF16)
    q = jnp.dot(hx, wxq_ref[...], preferred_element_type=F32).astype(BF16)
    scale = HEAD_DIM ** -0.5
    heads = []
    for h in range(XA_HEADS):
        hs = slice(h * HEAD_DIM, (h + 1) * HEAD_DIM)
        k_h = kv_ref[:, hs]
        v_h = kv_ref[:, XA_WIDTH + h * HEAD_DIM:XA_WIDTH + (h + 1) * HEAD_DIM]
        s = lax.dot_general(q[:, hs], k_h, (((1,), (1,)), ((), ())), preferred_element_type=F32) * scale
        e = jnp.exp(s - jnp.max(s, axis=-1, keepdims=True))
        p = e * (1.0 / jnp.sum(e, axis=-1, keepdims=True))
        heads.append(jnp.dot(p.astype(BF16), v_h, preferred_element_type=F32).astype(BF16))
    x2 = x1 + jnp.dot(jnp.concatenate(heads, axis=1), wxo_ref[...], preferred_element_type=F32)

    if last:
        out_refs[0][...] = _rms(x2, nn_ref[...])
    else:
        out_refs[0][...] = x2
        out_refs[1][...] = _rms(x2, nn_ref[...]).astype(BF16)


def _post(x, m_sb, m_gla, m_sgu, w_out_bf, norm_x3, w_xq_bf, kv, w_xo_bf, next_norm3, layer, next_idx, last):
    tm = POST_TM
    per_b = SEQ // tm
    row_spec = lambda width: pl.BlockSpec((tm, width), lambda i: (i, 0))
    const = lambda *shape: pl.BlockSpec((None,) + shape, lambda i: (layer,) + (0,) * len(shape),
                                        pipeline_mode=pl.Buffered(1))
    x_out = jax.ShapeDtypeStruct((TOKENS, D_MODEL), F32)
    if last:
        out_shape, out_specs = x_out, row_spec(D_MODEL)
    else:
        out_shape = (x_out, jax.ShapeDtypeStruct((TOKENS, D_MODEL), BF16))
        out_specs = (row_spec(D_MODEL), row_spec(D_MODEL))
    return pl.pallas_call(
        functools.partial(_post_kernel, last=last),
        grid=(TOKENS // tm,),
        in_specs=[row_spec(D_MODEL), row_spec(SB_WIDTH), row_spec(GLA_WIDTH), row_spec(SGU_WIDTH),
                  const(D_MODEL, D_MODEL),
                  const(1, D_MODEL),
                  const(D_MODEL, XA_WIDTH),
                  pl.BlockSpec((None, N_MEM, 2 * XA_WIDTH), lambda i: (layer, i // per_b, 0)),
                  const(XA_WIDTH, D_MODEL),
                  pl.BlockSpec((None, 1, D_MODEL), lambda i: (next_idx, 0, 0))],
        out_specs=out_specs,
        out_shape=out_shape,
        compiler_params=_params(("parallel",)),
        name="post",
    )(x, m_sb, m_gla, m_sgu, w_out_bf, norm_x3, w_xq_bf, kv, w_xo_bf, next_norm3)


def _constants():
    r = jnp.arange(2 * SB_TK)[:, None]
    c = jnp.arange(2 * SB_TK)[None, :]
    tt = jnp.where(c < SB_TK, (r % SB_TK) > c, True).astype(BF16)
    t = jnp.arange(GLA_RB)
    tril = ((t[:, None] >= t[None, :]) & (t[:, None] // GLA_CHUNK == t[None, :] // GLA_CHUNK)).astype(BF16)
    row_chunk = (jnp.arange(GLA_OFF_ROWS)[:, None] // GLA_SUB) % GLA_NCH
    offmask = jnp.stack([(row_chunk == t[None, :] // GLA_CHUNK) & (t[None, :] % GLA_CHUNK < i * GLA_SUB)
                         for i in range(1, GLA_CHUNK // GLA_SUB)]).astype(F32)
    e_row = jnp.arange(GLA_SUB * GLA_KEY_WIDTH)
    e_lane = GLA_SUB * ((e_row % GLA_KEY_WIDTH) // GLA_HEAD_K) + (GLA_SUB - 1) - e_row // GLA_KEY_WIDTH
    e_mat = (e_lane[:, None] == jnp.arange(LANES)[None, :]).astype(BF16)
    bmask = (jnp.arange(2 * HEAD_DIM)[:, None] // HEAD_DIM
             == jnp.arange(GLA_PAIR)[None, :] // GLA_HEAD_K).astype(F32)
    return tt, tril, e_mat, bmask, offmask


def kernel(x, mem, norm_mix, w_in, w_gla_gate_up, b_gla_gate, sgu_norm, w_sgu, b_sgu, out_norm, w_out,
           norm_xattn, norm_mem, w_xq, w_xkv, w_xo, final_norm):
    tt, tril, e_mat, bmask, offmask = _constants()
    w_in_t = jnp.swapaxes(w_in, 1, 2)
    wup_p = jnp.pad(w_gla_gate_up, ((0, 0), (0, LANES - GLA_GATE_RANK), (0, 0))).astype(BF16)
    w_out_bf, w_xq_bf, w_xo_bf = (w.astype(BF16) for w in (w_out, w_xq, w_xo))
    row3 = lambda a: a.reshape(a.shape[0], 1, a.shape[-1])
    norm_mix3, bup3, sgu_norm3, out_norm3 = row3(norm_mix), row3(b_gla_gate), row3(sgu_norm), row3(out_norm)
    norm_x3, norm_mem3 = row3(norm_xattn), row3(norm_mem)
    final3 = final_norm.reshape(1, 1, D_MODEL)
    b_sgu_b = jnp.broadcast_to(b_sgu[..., None], b_sgu.shape + (HEAD_DIM,))

    xf = x.reshape(TOKENS, D_MODEL)
    kv = _mem_kv(mem.reshape(BATCH * N_MEM, D_MODEL), norm_mem3, w_xkv)
    h = _norm_rows(xf, norm_mix3, 0)
    for l in range(DEPTH):
        last = l == DEPTH - 1
        proj = _in_proj_a(h, w_in_t, l)
        proj_b = _in_proj_b(h, w_in_t, l)
        m_sb = _sb_attention(proj, tt, out_norm3, l)
        m_gla = _gla(proj, proj_b, wup_p, bup3, tril, e_mat, bmask, offmask, out_norm3, l)
        m_sgu = _sgu(proj_b, w_sgu, b_sgu_b, sgu_norm3, out_norm3, l)
        nxt = final3 if last else norm_mix3
        res = _post(xf, m_sb, m_gla, m_sgu, w_out_bf, norm_x3, w_xq_bf, kv, w_xo_bf, nxt,
                    l, 0 if last else l + 1, last)
        if last:
            xf = res
        else:
            xf, h = res
    return xf.reshape(BATCH, SEQ, D_MODEL)
```

```python
import functools

import jax
import jax.numpy as jnp
from jax import lax
from jax.experimental import pallas as pl
from jax.experimental.pallas import tpu as pltpu

F32 = jnp.float32
BF16 = jnp.bfloat16

D_MODEL = 2048
BATCH = 4
SEQ = 2048
DEPTH = 4
TOKENS = BATCH * SEQ
HEAD_DIM = 128
SB_HEADS = 8
SB_WIDTH = SB_HEADS * HEAD_DIM
GLA_HEADS = 4
GLA_HEAD_K = 64
GLA_WIDTH = GLA_HEADS * HEAD_DIM
GLA_KEY_WIDTH = GLA_HEADS * GLA_HEAD_K
GLA_GATE_RANK = 16
GLA_GATE_TAU = 16.0
GLA_CHUNK = 64
GLA_SUB = 8
SGU_GROUPS = 4
SGU_WIDTH = 512
SGU_CHUNK = 128
N_MEM = 256
XA_HEADS = 4
XA_WIDTH = XA_HEADS * HEAD_DIM
EPS = 1e-6

LANES = 128

COL_SB_Q = 0
COL_SB_K = SB_WIDTH
COL_SB_V = 2 * SB_WIDTH
COL_SB_G = 3 * SB_WIDTH
COL_GLA_Q = 4 * SB_WIDTH
COL_GLA_K = COL_GLA_Q + GLA_KEY_WIDTH
COL_GLA_V = COL_GLA_K + GLA_KEY_WIDTH
PROJ_A_WIDTH = COL_GLA_V + GLA_WIDTH
COLB_GLA_G = 0
COLB_SGU_U = GLA_WIDTH
COLB_SGU_V = COLB_SGU_U + SGU_WIDTH
COLB_SGU_G = COLB_SGU_V + SGU_WIDTH
COLB_GLA_R = COLB_SGU_G + SGU_WIDTH
PROJ_B_MAIN = COLB_GLA_R
PROJ_B_WIDTH = COLB_GLA_R + LANES
ORIG_GLA_R = PROJ_A_WIDTH

VMEM_LIMIT = 56 * 1024 * 1024


def _params(semantics):
    return pltpu.CompilerParams(dimension_semantics=semantics, vmem_limit_bytes=VMEM_LIMIT)


def _rms(x, g):
    ms = jnp.mean(x * x, axis=-1, keepdims=True)
    return x * lax.rsqrt(ms + EPS) * g


def _silu(g):
    half = 0.5 * g
    return half + half * jnp.tanh(half)


def _softplus(z):
    return jnp.maximum(z, 0.0) + jnp.log(1.0 + jnp.exp(-jnp.abs(z)))


def _split_bf16(x):
    hi = x.astype(BF16)
    lo = (x - hi.astype(F32)).astype(BF16)
    return hi, lo


def _gelu_tanh(x):
    c = 0.7978845608028654
    half = 0.5 * x
    return half + half * jnp.tanh(x * (c * 0.044715 * (x * x) + c))


def _norm_kernel(x_ref, g_ref, o_ref):
    o_ref[...] = _rms(x_ref[...], g_ref[...]).astype(o_ref.dtype)


def _norm_rows(x2d, g3d, layer, tm=1024):
    rows = x2d.shape[0]
    return pl.pallas_call(
        _norm_kernel,
        grid=(rows // tm,),
        in_specs=[pl.BlockSpec((tm, D_MODEL), lambda i: (i, 0)),
                  pl.BlockSpec((None, 1, D_MODEL), lambda i: (layer, 0, 0))],
        out_specs=pl.BlockSpec((tm, D_MODEL), lambda i: (i, 0)),
        out_shape=jax.ShapeDtypeStruct((rows, D_MODEL), BF16),
        compiler_params=_params(("parallel",)),
        name="rmsnorm_rows",
    )(x2d, g3d)


def _memkv_kernel(m_ref, g_ref, w_ref, o_ref):
    mn = _rms(m_ref[...], g_ref[...]).astype(BF16)
    o_ref[...] = jnp.dot(mn, w_ref[...].astype(BF16), preferred_element_type=F32).astype(o_ref.dtype)


def _mem_kv(mem2d, norm_mem3, w_xkv):
    rows = mem2d.shape[0]
    return pl.pallas_call(
        _memkv_kernel,
        grid=(DEPTH,),
        in_specs=[pl.BlockSpec((rows, D_MODEL), lambda l: (0, 0)),
                  pl.BlockSpec((None, 1, D_MODEL), lambda l: (l, 0, 0)),
                  pl.BlockSpec((None, D_MODEL, 2 * XA_WIDTH), lambda l: (l, 0, 0))],
        out_specs=pl.BlockSpec((None, rows, 2 * XA_WIDTH), lambda l: (l, 0, 0)),
        out_shape=jax.ShapeDtypeStruct((DEPTH, rows, 2 * XA_WIDTH), BF16),
        compiler_params=_params(("parallel",)),
        name="mem_kv",
    )(mem2d, norm_mem3, w_xkv)


W_CONV_ROWS = 256


def _store_transposed_bf16(wt_ref, wbf_ref, n_rows):
    def conv(i, c):
        r0 = pl.multiple_of(i * W_CONV_ROWS, W_CONV_ROWS)
        wbf_ref[:, pl.ds(r0, W_CONV_ROWS)] = wt_ref[pl.ds(r0, W_CONV_ROWS), :].T.astype(BF16)
        return c

    lax.fori_loop(0, n_rows // W_CONV_ROWS, conv, 0)


def _proj_a_kernel(h_ref, wt_ref, o_ref, wbf_ref):
    @pl.when(pl.program_id(1) == 0)
    def _():
        _store_transposed_bf16(wt_ref, wbf_ref, wt_ref.shape[0])

    o_ref[...] = jnp.dot(h_ref[...], wbf_ref[...], preferred_element_type=F32).astype(o_ref.dtype)


def _in_proj_a(h, w_in_t, layer, tm=1024, tn=1280):
    return pl.pallas_call(
        _proj_a_kernel,
        grid=(PROJ_A_WIDTH // tn, TOKENS // tm),
        in_specs=[pl.BlockSpec((tm, D_MODEL), lambda n, m: (m, 0)),
                  pl.BlockSpec((None, tn, D_MODEL), lambda n, m: (layer, n, 0))],
        out_specs=pl.BlockSpec((tm, tn), lambda n, m: (m, n)),
        out_shape=jax.ShapeDtypeStruct((TOKENS, PROJ_A_WIDTH), BF16),
        scratch_shapes=[pltpu.VMEM((D_MODEL, tn), BF16)],
        compiler_params=_params(("parallel", "arbitrary")),
        name="in_proj_a",
    )(h, w_in_t)


def _proj_b_kernel(h_ref, wt_ref, wr_ref, o_ref, wbf_ref):
    @pl.when(pl.program_id(0) == 0)
    def _():
        _store_transposed_bf16(wt_ref, wbf_ref, PROJ_B_MAIN)
        rank = jnp.concatenate([wr_ref[...], jnp.zeros((LANES - GLA_GATE_RANK, D_MODEL), F32)], axis=0)
        wbf_ref[:, PROJ_B_MAIN:] = rank.T.astype(BF16)

    o_ref[...] = jnp.dot(h_ref[...], wbf_ref[...], preferred_element_type=F32).astype(o_ref.dtype)


def _in_proj_b(h, w_in_t, layer, tm=1024):
    once = pl.Buffered(1)
    window = lambda rows, start: pl.BlockSpec((None, pl.Element(rows), pl.Element(D_MODEL)),
                                              lambda m: (layer, start, 0), pipeline_mode=once)
    return pl.pallas_call(
        _proj_b_kernel,
        grid=(TOKENS // tm,),
        in_specs=[pl.BlockSpec((tm, D_MODEL), lambda m: (m, 0)),
                  window(PROJ_B_MAIN, ORIG_GLA_R + GLA_GATE_RANK),
                  window(GLA_GATE_RANK, ORIG_GLA_R)],
        out_specs=pl.BlockSpec((tm, PROJ_B_WIDTH), lambda m: (m, 0)),
        out_shape=jax.ShapeDtypeStruct((TOKENS, PROJ_B_WIDTH), BF16),
        scratch_shapes=[pltpu.VMEM((D_MODEL, PROJ_B_WIDTH), BF16)],
        compiler_params=_params(("arbitrary",)),
        name="in_proj_b",
    )(h, w_in_t, w_in_t)


SB_TQ = 256
SB_TK = 128
SB_NQ = SEQ // SB_TQ
SB_QB = SB_TQ // SB_TK
SB_DONE = -88.0
SB_HG = 8
SB_GW = SB_HG * HEAD_DIM


def _sb_suffix(log_keep, tt):
    hi, lo = _split_bf16(log_keep)
    return jnp.dot(jnp.concatenate([hi, lo], axis=1), tt, preferred_element_type=F32)


def _sb_own_rows(z, v, tt, causal):
    sp = _softplus(z)
    log_keep = jnp.where(causal, -sp, 0.0)
    cs_r = _sb_suffix(log_keep[:, SB_TK:], tt)
    cs_l = _sb_suffix(log_keep[:, :SB_TK], tt)
    log_beta = z - sp
    carry_l = cs_r[:, SB_TK:]
    a = jnp.exp(jnp.concatenate([log_beta[:, :SB_TK] + cs_l[:, :SB_TK] + carry_l,
                                 log_beta[:, SB_TK:] + cs_r[:, :SB_TK]], axis=1))
    a = jnp.where(causal, a, 0.0)
    return jnp.dot(a.astype(BF16), v, preferred_element_type=F32), carry_l + cs_l[:, SB_TK:]


def _sb_block(z, tt, carry):
    sp = _softplus(z)
    cs = _sb_suffix(-sp, tt)
    return jnp.exp((z - sp) + cs[:, :SB_TK] + carry), carry + cs[:, SB_TK:]


def _sb_kernel(q_ref, k_ref, v_ref, g_ref, tt_ref, gn_ref, o_ref, acc_ref, carry_ref, zd_ref, z1_ref):
    scale = HEAD_DIM ** -0.5
    nt = (((1,), (1,)), ((), ()))
    heads = [slice(i * HEAD_DIM, (i + 1) * HEAD_DIM) for i in range(SB_HG)]
    causal = (lax.broadcasted_iota(jnp.int32, (SB_TQ, SB_TQ), 1)
              < lax.broadcasted_iota(jnp.int32, (SB_TQ, SB_TQ), 0))

    def key_rows(first_block, n_blocks):
        if isinstance(first_block, int):
            return slice(first_block * SB_TK, (first_block + n_blocks) * SB_TK)
        return pl.ds(pl.multiple_of(first_block * SB_TK, SB_TK), n_blocks * SB_TK)

    def tile_rows(t):
        return key_rows(t * SB_QB, SB_QB)

    def logits(i, qt, first_block, n_blocks):
        return lax.dot_general(q_ref[tile_rows(qt), heads[i]], k_ref[key_rows(first_block, n_blocks), heads[i]],
                               nt, preferred_element_type=F32) * scale

    def finish(qi, which):
        for i in which:
            y = _rms(acc_ref[i], gn_ref[:, heads[i]])
            gate = g_ref[tile_rows(qi), heads[i]].astype(F32)
            o_ref[tile_rows(qi), heads[i]] = (y * _silu(gate)).astype(o_ref.dtype)

    def prefetch_next_tile(i, qi):
        nxt = jnp.minimum(qi + 1, SB_NQ - 1)
        zd_ref[i] = logits(i, nxt, nxt * SB_QB, SB_QB)
        z1_ref[i] = logits(i, nxt, nxt * SB_QB - 1, 1)

    def tile(qi, c):
        worst = []
        for i in range(SB_HG):
            zd, z1 = zd_ref[i], z1_ref[i]
            prefetch_next_tile(i, qi)
            pv, carry = _sb_own_rows(zd, v_ref[tile_rows(qi), heads[i]], tt_ref[...], causal)
            a, carry = _sb_block(z1, tt_ref[...], carry)
            acc_ref[i] = pv + jnp.dot(a.astype(BF16), v_ref[key_rows(qi * SB_QB - 1, 1), heads[i]],
                                      preferred_element_type=F32)
            carry_ref[i] = carry
            worst.append(jnp.max(carry))

        def more(state):
            kb, *left = state
            return jnp.logical_and(kb >= 0, functools.reduce(jnp.maximum, left) > SB_DONE)

        def block_step(state):
            kb, *left = state

            def head_step(i):
                a, carry = _sb_block(logits(i, qi, kb, 1), tt_ref[...], carry_ref[i])
                acc_ref[i] = acc_ref[i] + jnp.dot(a.astype(BF16), v_ref[key_rows(kb, 1), heads[i]],
                                                  preferred_element_type=F32)
                carry_ref[i] = carry
                return jnp.max(carry)

            return (kb - 1, *[lax.cond(left[i] > SB_DONE, functools.partial(head_step, i), lambda i=i: left[i])
                              for i in range(SB_HG)])

        finish(qi, range(SB_HG))
        lax.while_loop(more, block_step, (qi * SB_QB - 2, *worst))
        for i in range(SB_HG):
            @pl.when(worst[i] > SB_DONE)
            def _(i=i):
                finish(qi, [i])

        return c

    for i in range(SB_HG):
        pv, _ = _sb_own_rows(logits(i, 0, 0, SB_QB), v_ref[0:SB_TQ, heads[i]], tt_ref[...], causal)
        acc_ref[i] = pv
        prefetch_next_tile(i, 0)
    finish(0, range(SB_HG))
    lax.fori_loop(1, SB_NQ, tile, 0)


def _sb_attention(proj, tt, out_norm3, layer):
    group_cols = lambda first: pl.BlockSpec((SEQ, SB_GW), lambda b, h: (b, first // SB_GW + h))
    return pl.pallas_call(
        _sb_kernel,
        grid=(BATCH, SB_HEADS // SB_HG),
        in_specs=[group_cols(COL_SB_Q), group_cols(COL_SB_K), group_cols(COL_SB_V), group_cols(COL_SB_G),
                  pl.BlockSpec((2 * SB_TK, 2 * SB_TK), lambda b, h: (0, 0)),
                  pl.BlockSpec((None, 1, SB_GW), lambda b, h: (layer, 0, h))],
        out_specs=group_cols(0),
        out_shape=jax.ShapeDtypeStruct((TOKENS, SB_WIDTH), BF16),
        scratch_shapes=[pltpu.VMEM((SB_HG, SB_TQ, HEAD_DIM), F32), pltpu.VMEM((SB_HG, SB_TQ, SB_TK), F32),
                        pltpu.VMEM((SB_HG, SB_TQ, SB_TQ), F32), pltpu.VMEM((SB_HG, SB_TQ, SB_TK), F32)],
        compiler_params=_params(("parallel", "parallel")),
        name="sb_attention",
    )(proj, proj, proj, proj, tt, out_norm3)


GLA_RB = 256
GLA_NCH = GLA_RB // GLA_CHUNK
GLA_OFF_ROWS = GLA_HEADS * GLA_NCH * GLA_SUB
GLA_PAIR = 2 * GLA_HEAD_K


def _gla_kernel(q_ref, k_ref, v_ref, r_ref, g_ref, wup_ref, bup_ref, tril_ref, e_ref, bm_ref, om_ref, gn_ref,
                o_ref, st_ref):
    @pl.when(pl.program_id(1) == 0)
    def _():
        st_ref[...] = jnp.zeros_like(st_ref)

    R, C, S = GLA_RB, GLA_CHUNK, GLA_SUB
    chunks = [slice(c * C, (c + 1) * C) for c in range(GLA_NCH)]
    lane_k = lax.broadcasted_iota(jnp.int32, (1, GLA_KEY_WIDTH), 1)
    head_masks = [(lane_k // GLA_HEAD_K == h).astype(F32) for h in range(GLA_HEADS)]
    nt = (((1,), (1,)), ((), ()))
    tn = (((0,), (0,)), ((), ()))

    q = q_ref[...].astype(F32) * (GLA_HEAD_K ** -0.5)
    k = k_ref[...].astype(F32)
    v = v_ref[...]
    logits = jnp.dot(r_ref[...], wup_ref[...], preferred_element_type=F32) + bup_ref[...]
    log_alpha = (jnp.minimum(logits, 0.0) - jnp.log(1.0 + jnp.exp(-jnp.abs(logits)))) * (1.0 / GLA_GATE_TAU)
    hi, lo = _split_bf16(log_alpha)
    tril = tril_ref[...]
    bc = (jnp.dot(tril, hi, preferred_element_type=F32)
          + jnp.dot(tril, lo, preferred_element_type=F32))
    b_last = [bc[ch.stop - 1:ch.stop, :] for ch in chunks]

    q_dec = (q * jnp.exp(bc)).astype(BF16)
    k_dec = jnp.concatenate([k[ch] * jnp.exp(b_last[c] - bc[ch]) for c, ch in enumerate(chunks)],
                            axis=0).astype(BF16)
    o_inter = [[None] * (GLA_HEADS // 2) for _ in chunks]
    states = [st_ref[p] for p in range(GLA_HEADS // 2)]

    def recur_step(p, c):
        ch = chunks[c]
        ks = slice(p * GLA_PAIR, (p + 1) * GLA_PAIR)
        vs = slice(p * 2 * HEAD_DIM, (p + 1) * 2 * HEAD_DIM)
        st = states[p]
        o_inter[c][p] = lax.dot_general(q_dec[ch, ks], st.astype(BF16), nt, preferred_element_type=F32)
        upd = lax.dot_general(v[ch, vs], k_dec[ch, ks], tn, preferred_element_type=F32)
        states[p] = st * jnp.exp(b_last[c][:, ks]) + upd * bm_ref[...]

    off = []

    def off_step(i):
        q_parts, k_parts = [], []
        for ch in chunks:
            lo_r = ch.start + i * S
            b_row = bc[lo_r - 1:lo_r, :]
            q_parts.append(q[lo_r:lo_r + S, :] * jnp.exp(bc[lo_r:lo_r + S, :] - b_row))
            k_parts.append(k[ch] * jnp.exp(jnp.minimum(b_row - bc[ch], 0.0)))
        q_i = jnp.concatenate(q_parts, axis=0)
        k_i = jnp.concatenate(k_parts, axis=0).astype(BF16)
        lhs = jnp.concatenate([q_i * head_masks[h] for h in range(GLA_HEADS)], axis=0).astype(BF16)
        off.append(lax.dot_general(lhs, k_i, nt, preferred_element_type=F32) * om_ref[i - 1])

    def shifted(x, d):
        return pltpu.roll(x.reshape(R // S, S, GLA_KEY_WIDTH), d, 1).reshape(R, GLA_KEY_WIDTH)

    tmod = lax.broadcasted_iota(jnp.int32, (R, GLA_KEY_WIDTH), 0) % S
    diag = [(q * k).astype(BF16)]

    def diag_step(d):
        prod = q * shifted(k, d) * jnp.exp(bc - shifted(bc, d))
        diag.append(jnp.where(tmod >= d, prod, 0.0).astype(BF16))

    vector_work = ([functools.partial(off_step, i) for i in range(1, C // S)]
                   + [functools.partial(diag_step, d) for d in range(1, S)])
    chain = [functools.partial(recur_step, p, c) for c in range(GLA_NCH) for p in range(GLA_HEADS // 2)]
    emitted = 0
    for j, work in enumerate(vector_work):
        while emitted < len(chain) and emitted * len(vector_work) <= j * len(chain):
            chain[emitted]()
            emitted += 1
        work()
    for step in chain[emitted:]:
        step()
    for p in range(GLA_HEADS // 2):
        st_ref[p] = states[p]
    o = jnp.concatenate([jnp.concatenate(parts, axis=1) for parts in o_inter], axis=0)
    score = jnp.dot(jnp.concatenate(diag, axis=1), e_ref[...], preferred_element_type=F32)

    lane_head = lax.broadcasted_iota(jnp.int32, (C, LANES), 1) // S
    no_diag = jnp.zeros((C, LANES), F32)
    o_intra = []
    for h in range(GLA_HEADS):
        blocks = []
        for c, ch in enumerate(chunks):
            a_diag = pltpu.roll(jnp.where(lane_head == h, score[ch], 0.0),
                                (LANES - (S - 1) - S * h + C * c) % LANES, 1, stride=1, stride_axis=0)
            halves = [no_diag] * (R // LANES)
            halves[C * c // LANES] = a_diag
            first = (h * GLA_NCH + c) * S
            a_off = jnp.concatenate([jnp.zeros((S, R), F32)] + [a[first:first + S, :] for a in off], axis=0)
            blocks.append(a_off + jnp.concatenate(halves, axis=1))
        a_h = jnp.concatenate(blocks, axis=0).astype(BF16)
        o_intra.append(jnp.dot(a_h, v[:, h * HEAD_DIM:(h + 1) * HEAD_DIM], preferred_element_type=F32))
    o = o + jnp.concatenate(o_intra, axis=1)

    g = g_ref[...].astype(F32)
    for h in range(GLA_HEADS):
        hs = slice(h * HEAD_DIM, (h + 1) * HEAD_DIM)
        y = _rms(o[:, hs], gn_ref[:, hs])
        o_ref[:, hs] = (y * _silu(g[:, hs])).astype(o_ref.dtype)


def _gla(proj, proj_b, wup_p, bup3, tril, e_mat, bmask, offmask, out_norm3, layer):
    nr = SEQ // GLA_RB
    row = lambda b, r: b * nr + r
    return pl.pallas_call(
        _gla_kernel,
        grid=(BATCH, nr),
        in_specs=[pl.BlockSpec((GLA_RB, GLA_KEY_WIDTH), lambda b, r: (row(b, r), COL_GLA_Q // GLA_KEY_WIDTH)),
                  pl.BlockSpec((GLA_RB, GLA_KEY_WIDTH), lambda b, r: (row(b, r), COL_GLA_K // GLA_KEY_WIDTH)),
                  pl.BlockSpec((GLA_RB, GLA_WIDTH), lambda b, r: (row(b, r), COL_GLA_V // GLA_WIDTH)),
                  pl.BlockSpec((GLA_RB, LANES), lambda b, r: (row(b, r), COLB_GLA_R // LANES)),
                  pl.BlockSpec((GLA_RB, GLA_WIDTH), lambda b, r: (row(b, r), COLB_GLA_G // GLA_WIDTH)),
                  pl.BlockSpec((None, LANES, GLA_KEY_WIDTH), lambda b, r: (layer, 0, 0)),
                  pl.BlockSpec((None, 1, GLA_KEY_WIDTH), lambda b, r: (layer, 0, 0)),
                  pl.BlockSpec((GLA_RB, GLA_RB), lambda b, r: (0, 0)),
                  pl.BlockSpec((GLA_SUB * GLA_KEY_WIDTH, LANES), lambda b, r: (0, 0)),
                  pl.BlockSpec((2 * HEAD_DIM, GLA_PAIR), lambda b, r: (0, 0)),
                  pl.BlockSpec((GLA_CHUNK // GLA_SUB - 1, GLA_OFF_ROWS, GLA_RB), lambda b, r: (0, 0, 0)),
                  pl.BlockSpec((None, 1, GLA_WIDTH), lambda b, r: (layer, 0, SB_WIDTH // GLA_WIDTH))],
        out_specs=pl.BlockSpec((GLA_RB, GLA_WIDTH), lambda b, r: (row(b, r), 0)),
        out_shape=jax.ShapeDtypeStruct((TOKENS, GLA_WIDTH), BF16),
        scratch_shapes=[pltpu.VMEM((GLA_HEADS // 2, 2 * HEAD_DIM, GLA_PAIR), F32)],
        compiler_params=_params(("parallel", "arbitrary")),
        name="gla",
    )(proj, proj, proj, proj_b, proj_b, wup_p, bup3, tril, e_mat, bmask, offmask, out_norm3)


SGU_RB = 1024


def _sgu_kernel(u_ref, v_ref, g_ref, w_ref, bb_ref, sn_ref, gn_ref, o_ref):
    C = SGU_CHUNK
    keep = (lax.broadcasted_iota(jnp.int32, (C, C), 0) >= lax.broadcasted_iota(jnp.int32, (C, C), 1))
    w = [jnp.where(keep, w_ref[g], 0.0).astype(BF16) for g in range(SGU_GROUPS)]
    for c in range(SGU_RB // C):
        rows = slice(c * C, (c + 1) * C)
        u = _gelu_tanh(u_ref[rows, :].astype(F32))
        v = _gelu_tanh(v_ref[rows, :].astype(F32))
        vn = _rms(v, sn_ref[...]).astype(BF16)
        gate = g_ref[rows, :].astype(F32)
        for g in range(SGU_GROUPS):
            gs = slice(g * HEAD_DIM, (g + 1) * HEAD_DIM)
            mixed = jnp.dot(w[g], vn[:, gs], preferred_element_type=F32) + bb_ref[g]
            y = _rms(u[:, gs] * mixed, gn_ref[:, gs])
            o_ref[rows, gs] = (y * _silu(gate[:, gs])).astype(o_ref.dtype)


def _sgu(proj, w_sgu, b_sgu_b, sgu_norm3, out_norm3, layer):
    return pl.pallas_call(
        _sgu_kernel,
        grid=(TOKENS // SGU_RB,),
        in_specs=[pl.BlockSpec((SGU_RB, SGU_WIDTH), lambda i: (i, COLB_SGU_U // SGU_WIDTH)),
                  pl.BlockSpec((SGU_RB, SGU_WIDTH), lambda i: (i, COLB_SGU_V // SGU_WIDTH)),
                  pl.BlockSpec((SGU_RB, SGU_WIDTH), lambda i: (i, COLB_SGU_G // SGU_WIDTH)),
                  pl.BlockSpec((None, SGU_GROUPS, SGU_CHUNK, SGU_CHUNK), lambda i: (layer, 0, 0, 0)),
                  pl.BlockSpec((None, SGU_GROUPS, SGU_CHUNK, HEAD_DIM), lambda i: (layer, 0, 0, 0)),
                  pl.BlockSpec((None, 1, SGU_WIDTH), lambda i: (layer, 0, 0)),
                  pl.BlockSpec((None, 1, SGU_WIDTH), lambda i: (layer, 0, (SB_WIDTH + GLA_WIDTH) // SGU_WIDTH))],
        out_specs=pl.BlockSpec((SGU_RB, SGU_WIDTH), lambda i: (i, 0)),
        out_shape=jax.ShapeDtypeStruct((TOKENS, SGU_WIDTH), BF16),
        compiler_params=_params(("parallel",)),
        name="sgu",
    )(proj, proj, proj, w_sgu, b_sgu_b, sgu_norm3, out_norm3)


POST_TM = 512


def _post_kernel(x_ref, msb_ref, mgla_ref, msgu_ref, wout_ref, nx_ref, wxq_ref, kv_ref, wxo_ref, nn_ref,
                 *out_refs, last):
    mix = jnp.concatenate([msb_ref[...], mgla_ref[...], msgu_ref[...]], axis=1)
    x1 = x_ref[...] + jnp.dot(mix, wout_ref[...], preferred_element_type=F32)

    hx = _rms(x1, nx_ref[...]).astype(BF16)
    q = jnp.dot(hx, wxq_ref[...], preferred_element_type=F32).astype(BF16)
    scale = HEAD_DIM ** -0.5
    heads = []
    for h in range(XA_HEADS):
        hs = slice(h * HEAD_DIM, (h + 1) * HEAD_DIM)
        k_h = kv_ref[:, hs]
        v_h = kv_ref[:, XA_WIDTH + h * HEAD_DIM:XA_WIDTH + (h + 1) * HEAD_DIM]
        s = lax.dot_general(q[:, hs], k_h, (((1,), (1,)), ((), ())), preferred_element_type=F32) * scale
        e = jnp.exp(s - jnp.max(s, axis=-1, keepdims=True))
        p = e * (1.0 / jnp.sum(e, axis=-1, keepdims=True))
        heads.append(jnp.dot(p.astype(BF16), v_h, preferred_element_type=F32).astype(BF16))
    x2 = x1 + jnp.dot(jnp.concatenate(heads, axis=1), wxo_ref[...], preferred_element_type=F32)

    if last:
        out_refs[0][...] = _rms(x2, nn_ref[...])
    else:
        out_refs[0][...] = x2
        out_refs[1][...] = _rms(x2, nn_ref[...]).astype(BF16)


def _post(x, m_sb, m_gla, m_sgu, w_out_bf, norm_x3, w_xq_bf, kv, w_xo_bf, next_norm3, layer, next_idx, last):
    tm = POST_TM
    per_b = SEQ // tm
    row_spec = lambda width: pl.BlockSpec((tm, width), lambda i: (i, 0))
    const = lambda *shape: pl.BlockSpec((None,) + shape, lambda i: (layer,) + (0,) * len(shape),
                                        pipeline_mode=pl.Buffered(1))
    x_out = jax.ShapeDtypeStruct((TOKENS, D_MODEL), F32)
    if last:
        out_shape, out_specs = x_out, row_spec(D_MODEL)
    else:
        out_shape = (x_out, jax.ShapeDtypeStruct((TOKENS, D_MODEL), BF16))
        out_specs = (row_spec(D_MODEL), row_spec(D_MODEL))
    return pl.pallas_call(
        functools.partial(_post_kernel, last=last),
        grid=(TOKENS // tm,),
        in_specs=[row_spec(D_MODEL), row_spec(SB_WIDTH), row_spec(GLA_WIDTH), row_spec(SGU_WIDTH),
                  const(D_MODEL, D_MODEL),
                  const(1, D_MODEL),
                  const(D_MODEL, XA_WIDTH),
                  pl.BlockSpec((None, N_MEM, 2 * XA_WIDTH), lambda i: (layer, i // per_b, 0)),
                  const(XA_WIDTH, D_MODEL),
                  pl.BlockSpec((None, 1, D_MODEL), lambda i: (next_idx, 0, 0))],
        out_specs=out_specs,
        out_shape=out_shape,
        compiler_params=_params(("parallel",)),
        name="post",
    )(x, m_sb, m_gla, m_sgu, w_out_bf, norm_x3, w_xq_bf, kv, w_xo_bf, next_norm3)


def _constants():
    r = jnp.arange(2 * SB_TK)[:, None]
    c = jnp.arange(2 * SB_TK)[None, :]
    tt = jnp.where(c < SB_TK, (r % SB_TK) > c, True).astype(BF16)
    t = jnp.arange(GLA_RB)
    tril = ((t[:, None] >= t[None, :]) & (t[:, None] // GLA_CHUNK == t[None, :] // GLA_CHUNK)).astype(BF16)
    row_chunk = (jnp.arange(GLA_OFF_ROWS)[:, None] // GLA_SUB) % GLA_NCH
    offmask = jnp.stack([(row_chunk == t[None, :] // GLA_CHUNK) & (t[None, :] % GLA_CHUNK < i * GLA_SUB)
                         for i in range(1, GLA_CHUNK // GLA_SUB)]).astype(F32)
    e_row = jnp.arange(GLA_SUB * GLA_KEY_WIDTH)
    e_lane = GLA_SUB * ((e_row % GLA_KEY_WIDTH) // GLA_HEAD_K) + (GLA_SUB - 1) - e_row // GLA_KEY_WIDTH
    e_mat = (e_lane[:, None] == jnp.arange(LANES)[None, :]).astype(BF16)
    bmask = (jnp.arange(2 * HEAD_DIM)[:, None] // HEAD_DIM
             == jnp.arange(GLA_PAIR)[None, :] // GLA_HEAD_K).astype(F32)
    return tt, tril, e_mat, bmask, offmask


def kernel(x, mem, norm_mix, w_in, w_gla_gate_up, b_gla_gate, sgu_norm, w_sgu, b_sgu, out_norm, w_out,
           norm_xattn, norm_mem, w_xq, w_xkv, w_xo, final_norm):
    tt, tril, e_mat, bmask, offmask = _constants()
    w_in_t = jnp.swapaxes(w_in, 1, 2)
    wup_p = jnp.pad(w_gla_gate_up, ((0, 0), (0, LANES - GLA_GATE_RANK), (0, 0))).astype(BF16)
    w_out_bf, w_xq_bf, w_xo_bf = (w.astype(BF16) for w in (w_out, w_xq, w_xo))
    row3 = lambda a: a.reshape(a.shape[0], 1, a.shape[-1])
    norm_mix3, bup3, sgu_norm3, out_norm3 = row3(norm_mix), row3(b_gla_gate), row3(sgu_norm), row3(out_norm)
    norm_x3, norm_mem3 = row3(norm_xattn), row3(norm_mem)
    final3 = final_norm.reshape(1, 1, D_MODEL)
    b_sgu_b = jnp.broadcast_to(b_sgu[..., None], b_sgu.shape + (HEAD_DIM,))

    xf = x.reshape(TOKENS, D_MODEL)
    kv = _mem_kv(mem.reshape(BATCH * N_MEM, D_MODEL), norm_mem3, w_xkv)
    h = _norm_rows(xf, norm_mix3, 0)
    for l in range(DEPTH):
        last = l == DEPTH - 1
        proj = _in_proj_a(h, w_in_t, l)
        proj_b = _in_proj_b(h, w_in_t, l)
        m_sb = _sb_attention(proj, tt, out_norm3, l)
        m_gla = _gla(proj, proj_b, wup_p, bup3, tril, e_mat, bmask, offmask, out_norm3, l)
        m_sgu = _sgu(proj_b, w_sgu, b_sgu_b, sgu_norm3, out_norm3, l)
        nxt = final3 if last else norm_mix3
        res = _post(xf, m_sb, m_gla, m_sgu, w_out_bf, norm_x3, w_xq_bf, kv, w_xo_bf, nxt,
                    l, 0 if last else l + 1, last)
        if last:
            xf = res
        else:
            xf, h = res
    return xf.reshape(BATCH, SEQ, D_MODEL)
```

```python
import functools

import jax
import jax.numpy as jnp
from jax import lax
from jax.experimental import pallas as pl
from jax.experimental.pallas import tpu as pltpu

F32 = jnp.float32
BF16 = jnp.bfloat16

D_MODEL = 2048
BATCH = 4
SEQ = 2048
DEPTH = 4
TOKENS = BATCH * SEQ
HEAD_DIM = 128
SB_HEADS = 8
SB_WIDTH = SB_HEADS * HEAD_DIM
GLA_HEADS = 4
GLA_HEAD_K = 64
GLA_WIDTH = GLA_HEADS * HEAD_DIM
GLA_KEY_WIDTH = GLA_HEADS * GLA_HEAD_K
GLA_GATE_RANK = 16
GLA_GATE_TAU = 16.0
GLA_CHUNK = 64
GLA_SUB = 8
SGU_GROUPS = 4
SGU_WIDTH = 512
SGU_CHUNK = 128
N_MEM = 256
XA_HEADS = 4
XA_WIDTH = XA_HEADS * HEAD_DIM
EPS = 1e-6

LANES = 128

COL_SB_Q = 0
COL_SB_K = SB_WIDTH
COL_SB_V = 2 * SB_WIDTH
COL_SB_G = 3 * SB_WIDTH
COL_GLA_Q = 4 * SB_WIDTH
COL_GLA_K = COL_GLA_Q + GLA_KEY_WIDTH
COL_GLA_V = COL_GLA_K + GLA_KEY_WIDTH
PROJ_A_WIDTH = COL_GLA_V + GLA_WIDTH
COLB_GLA_G = 0
COLB_SGU_U = GLA_WIDTH
COLB_SGU_V = COLB_SGU_U + SGU_WIDTH
COLB_SGU_G = COLB_SGU_V + SGU_WIDTH
COLB_GLA_R = COLB_SGU_G + SGU_WIDTH
PROJ_B_MAIN = COLB_GLA_R
PROJ_B_WIDTH = COLB_GLA_R + LANES
ORIG_GLA_R = PROJ_A_WIDTH

VMEM_LIMIT = 56 * 1024 * 1024


def _params(semantics):
    return pltpu.CompilerParams(dimension_semantics=semantics, vmem_limit_bytes=VMEM_LIMIT)


def _rms(x, g):
    ms = jnp.mean(x * x, axis=-1, keepdims=True)
    return x * lax.rsqrt(ms + EPS) * g


def _silu(g):
    half = 0.5 * g
    return half + half * jnp.tanh(half)


def _softplus(z):
    return jnp.maximum(z, 0.0) + jnp.log(1.0 + jnp.exp(-jnp.abs(z)))


def _split_bf16(x):
    hi = x.astype(BF16)
    lo = (x - hi.astype(F32)).astype(BF16)
    return hi, lo


def _gelu_tanh(x):
    c = 0.7978845608028654
    half = 0.5 * x
    return half + half * jnp.tanh(x * (c * 0.044715 * (x * x) + c))


def _norm_kernel(x_ref, g_ref, o_ref):
    o_ref[...] = _rms(x_ref[...], g_ref[...]).astype(o_ref.dtype)


def _norm_rows(x2d, g3d, layer, tm=1024):
    rows = x2d.shape[0]
    return pl.pallas_call(
        _norm_kernel,
        grid=(rows // tm,),
        in_specs=[pl.BlockSpec((tm, D_MODEL), lambda i: (i, 0)),
                  pl.BlockSpec((None, 1, D_MODEL), lambda i: (layer, 0, 0))],
        out_specs=pl.BlockSpec((tm, D_MODEL), lambda i: (i, 0)),
        out_shape=jax.ShapeDtypeStruct((rows, D_MODEL), BF16),
        compiler_params=_params(("parallel",)),
        name="rmsnorm_rows",
    )(x2d, g3d)


def _memkv_kernel(m_ref, g_ref, w_ref, o_ref):
    mn = _rms(m_ref[...], g_ref[...]).astype(BF16)
    o_ref[...] = jnp.dot(mn, w_ref[...].astype(BF16), preferred_element_type=F32).astype(o_ref.dtype)


def _mem_kv(mem2d, norm_mem3, w_xkv):
    rows = mem2d.shape[0]
    return pl.pallas_call(
        _memkv_kernel,
        grid=(DEPTH,),
        in_specs=[pl.BlockSpec((rows, D_MODEL), lambda l: (0, 0)),
                  pl.BlockSpec((None, 1, D_MODEL), lambda l: (l, 0, 0)),
                  pl.BlockSpec((None, D_MODEL, 2 * XA_WIDTH), lambda l: (l, 0, 0))],
        out_specs=pl.BlockSpec((None, rows, 2 * XA_WIDTH), lambda l: (l, 0, 0)),
        out_shape=jax.ShapeDtypeStruct((DEPTH, rows, 2 * XA_WIDTH), BF16),
        compiler_params=_params(("parallel",)),
        name="mem_kv",
    )(mem2d, norm_mem3, w_xkv)


NT_DIMS = (((1,), (1,)), ((), ()))


def _proj_a_kernel(h_ref, wt_ref, o_ref, wbf_ref):
    @pl.when(pl.program_id(1) == 0)
    def _():
        wbf_ref[...] = wt_ref[...].astype(BF16)

    o_ref[...] = lax.dot_general(h_ref[...], wbf_ref[...], NT_DIMS,
                                 preferred_element_type=F32).astype(o_ref.dtype)


def _in_proj_a(h, w_in_t, layer, tm=1024, tn=1280):
    return pl.pallas_call(
        _proj_a_kernel,
        grid=(PROJ_A_WIDTH // tn, TOKENS // tm),
        in_specs=[pl.BlockSpec((tm, D_MODEL), lambda n, m: (m, 0)),
                  pl.BlockSpec((None, tn, D_MODEL), lambda n, m: (layer, n, 0))],
        out_specs=pl.BlockSpec((tm, tn), lambda n, m: (m, n)),
        out_shape=jax.ShapeDtypeStruct((TOKENS, PROJ_A_WIDTH), BF16),
        scratch_shapes=[pltpu.VMEM((tn, D_MODEL), BF16)],
        compiler_params=_params(("parallel", "arbitrary")),
        name="in_proj_a",
    )(h, w_in_t)


def _proj_b_kernel(h_ref, wt_ref, wr_ref, o_ref, wbf_ref):
    @pl.when(pl.program_id(0) == 0)
    def _():
        wbf_ref[:PROJ_B_MAIN, :] = wt_ref[...].astype(BF16)
        rank = jnp.concatenate([wr_ref[...], jnp.zeros((LANES - GLA_GATE_RANK, D_MODEL), F32)], axis=0)
        wbf_ref[PROJ_B_MAIN:, :] = rank.astype(BF16)

    o_ref[...] = lax.dot_general(h_ref[...], wbf_ref[...], NT_DIMS,
                                 preferred_element_type=F32).astype(o_ref.dtype)


def _in_proj_b(h, w_in_t, layer, tm=1024):
    once = pl.Buffered(1)
    window = lambda rows, start: pl.BlockSpec((None, pl.Element(rows), pl.Element(D_MODEL)),
                                              lambda m: (layer, start, 0), pipeline_mode=once)
    return pl.pallas_call(
        _proj_b_kernel,
        grid=(TOKENS // tm,),
        in_specs=[pl.BlockSpec((tm, D_MODEL), lambda m: (m, 0)),
                  window(PROJ_B_MAIN, ORIG_GLA_R + GLA_GATE_RANK),
                  window(GLA_GATE_RANK, ORIG_GLA_R)],
        out_specs=pl.BlockSpec((tm, PROJ_B_WIDTH), lambda m: (m, 0)),
        out_shape=jax.ShapeDtypeStruct((TOKENS, PROJ_B_WIDTH), BF16),
        scratch_shapes=[pltpu.VMEM((PROJ_B_WIDTH, D_MODEL), BF16)],
        compiler_params=_params(("arbitrary",)),
        name="in_proj_b",
    )(h, w_in_t, w_in_t)


SB_TQ = 256
SB_TK = 128
SB_NQ = SEQ // SB_TQ
SB_QB = SB_TQ // SB_TK
SB_DONE = -88.0
SB_HG = 4
SB_GW = SB_HG * HEAD_DIM


def _sb_suffix(log_keep, tt):
    hi, lo = _split_bf16(log_keep)
    return jnp.dot(jnp.concatenate([hi, lo], axis=1), tt, preferred_element_type=F32)


def _sb_own_rows(z, v, tt, causal):
    sp = _softplus(z)
    log_keep = jnp.where(causal, -sp, 0.0)
    cs_r = _sb_suffix(log_keep[:, SB_TK:], tt)
    cs_l = _sb_suffix(log_keep[:, :SB_TK], tt)
    log_beta = z - sp
    carry_l = cs_r[:, SB_TK:]
    a = jnp.exp(jnp.concatenate([log_beta[:, :SB_TK] + cs_l[:, :SB_TK] + carry_l,
                                 log_beta[:, SB_TK:] + cs_r[:, :SB_TK]], axis=1))
    a = jnp.where(causal, a, 0.0)
    return jnp.dot(a.astype(BF16), v, preferred_element_type=F32), carry_l + cs_l[:, SB_TK:]


def _sb_block(z, tt, carry):
    sp = _softplus(z)
    cs = _sb_suffix(-sp, tt)
    return jnp.exp((z - sp) + cs[:, :SB_TK] + carry), carry + cs[:, SB_TK:]


def _sb_kernel(q_ref, k_ref, v_ref, g_ref, tt_ref, gn_ref, o_ref, acc_ref, carry_ref, zd_ref, z1_ref):
    scale = HEAD_DIM ** -0.5
    nt = (((1,), (1,)), ((), ()))
    heads = [slice(i * HEAD_DIM, (i + 1) * HEAD_DIM) for i in range(SB_HG)]
    causal = (lax.broadcasted_iota(jnp.int32, (SB_TQ, SB_TQ), 1)
              < lax.broadcasted_iota(jnp.int32, (SB_TQ, SB_TQ), 0))

    def key_rows(first_block, n_blocks):
        if isinstance(first_block, int):
            return slice(first_block * SB_TK, (first_block + n_blocks) * SB_TK)
        return pl.ds(pl.multiple_of(first_block * SB_TK, SB_TK), n_blocks * SB_TK)

    def tile_rows(t):
        return key_rows(t * SB_QB, SB_QB)

    def logits(i, qt, first_block, n_blocks):
        return lax.dot_general(q_ref[tile_rows(qt), heads[i]], k_ref[key_rows(first_block, n_blocks), heads[i]],
                               nt, preferred_element_type=F32) * scale

    def finish(qi, which):
        for i in which:
            y = _rms(acc_ref[i], gn_ref[:, heads[i]])
            gate = g_ref[tile_rows(qi), heads[i]].astype(F32)
            o_ref[tile_rows(qi), heads[i]] = (y * _silu(gate)).astype(o_ref.dtype)

    def prefetch_next_tile(i, qi):
        nxt = jnp.minimum(qi + 1, SB_NQ - 1)
        zd_ref[i] = logits(i, nxt, nxt * SB_QB, SB_QB)
        z1_ref[i] = logits(i, nxt, nxt * SB_QB - 1, 1)

    def tile(qi, c):
        worst = []
        for i in range(SB_HG):
            zd, z1 = zd_ref[i], z1_ref[i]
            prefetch_next_tile(i, qi)
            pv, carry = _sb_own_rows(zd, v_ref[tile_rows(qi), heads[i]], tt_ref[...], causal)
            a, carry = _sb_block(z1, tt_ref[...], carry)
            acc_ref[i] = pv + jnp.dot(a.astype(BF16), v_ref[key_rows(qi * SB_QB - 1, 1), heads[i]],
                                      preferred_element_type=F32)
            carry_ref[i] = carry
            worst.append(jnp.max(carry))

        def more(state):
            kb, *left = state
            return jnp.logical_and(kb >= 0, functools.reduce(jnp.maximum, left) > SB_DONE)

        def block_step(state):
            kb, *left = state

            def head_step(i):
                a, carry = _sb_block(logits(i, qi, kb, 1), tt_ref[...], carry_ref[i])
                acc_ref[i] = acc_ref[i] + jnp.dot(a.astype(BF16), v_ref[key_rows(kb, 1), heads[i]],
                                                  preferred_element_type=F32)
                carry_ref[i] = carry
                return jnp.max(carry)

            return (kb - 1, *[lax.cond(left[i] > SB_DONE, functools.partial(head_step, i), lambda i=i: left[i])
                              for i in range(SB_HG)])

        finish(qi, range(SB_HG))
        lax.while_loop(more, block_step, (qi * SB_QB - 2, *worst))
        for i in range(SB_HG):
            @pl.when(worst[i] > SB_DONE)
            def _(i=i):
                finish(qi, [i])

        return c

    for i in range(SB_HG):
        pv, _ = _sb_own_rows(logits(i, 0, 0, SB_QB), v_ref[0:SB_TQ, heads[i]], tt_ref[...], causal)
        acc_ref[i] = pv
        prefetch_next_tile(i, 0)
    finish(0, range(SB_HG))
    lax.fori_loop(1, SB_NQ, tile, 0)


def _sb_attention(proj, tt, out_norm3, layer):
    group_cols = lambda first: pl.BlockSpec((SEQ, SB_GW), lambda b, h: (b, first // SB_GW + h))
    return pl.pallas_call(
        _sb_kernel,
        grid=(BATCH, SB_HEADS // SB_HG),
        in_specs=[group_cols(COL_SB_Q), group_cols(COL_SB_K), group_cols(COL_SB_V), group_cols(COL_SB_G),
                  pl.BlockSpec((2 * SB_TK, 2 * SB_TK), lambda b, h: (0, 0)),
                  pl.BlockSpec((None, 1, SB_GW), lambda b, h: (layer, 0, h))],
        out_specs=group_cols(0),
        out_shape=jax.ShapeDtypeStruct((TOKENS, SB_WIDTH), BF16),
        scratch_shapes=[pltpu.VMEM((SB_HG, SB_TQ, HEAD_DIM), F32), pltpu.VMEM((SB_HG, SB_TQ, SB_TK), F32),
                        pltpu.VMEM((SB_HG, SB_TQ, SB_TQ), F32), pltpu.VMEM((SB_HG, SB_TQ, SB_TK), F32)],
        compiler_params=_params(("parallel", "parallel")),
        name="sb_attention",
    )(proj, proj, proj, proj, tt, out_norm3)


GLA_RB = 256
GLA_NCH = GLA_RB // GLA_CHUNK
GLA_OFF_ROWS = GLA_HEADS * GLA_NCH * GLA_SUB
GLA_PAIR = 2 * GLA_HEAD_K


def _gla_kernel(q_ref, k_ref, v_ref, r_ref, g_ref, wup_ref, bup_ref, tril_ref, e_ref, bm_ref, om_ref, gn_ref,
                o_ref, st_ref):
    @pl.when(pl.program_id(1) == 0)
    def _():
        st_ref[...] = jnp.zeros_like(st_ref)

    R, C, S = GLA_RB, GLA_CHUNK, GLA_SUB
    chunks = [slice(c * C, (c + 1) * C) for c in range(GLA_NCH)]
    lane_k = lax.broadcasted_iota(jnp.int32, (1, GLA_KEY_WIDTH), 1)
    head_masks = [(lane_k // GLA_HEAD_K == h).astype(F32) for h in range(GLA_HEADS)]
    nt = (((1,), (1,)), ((), ()))
    tn = (((0,), (0,)), ((), ()))

    q = q_ref[...].astype(F32) * (GLA_HEAD_K ** -0.5)
    k = k_ref[...].astype(F32)
    v = v_ref[...]
    logits = jnp.dot(r_ref[...], wup_ref[...], preferred_element_type=F32) + bup_ref[...]
    log_alpha = (jnp.minimum(logits, 0.0) - jnp.log(1.0 + jnp.exp(-jnp.abs(logits)))) * (1.0 / GLA_GATE_TAU)
    hi, lo = _split_bf16(log_alpha)
    tril = tril_ref[...]
    bc = (jnp.dot(tril, hi, preferred_element_type=F32)
          + jnp.dot(tril, lo, preferred_element_type=F32))
    b_last = [bc[ch.stop - 1:ch.stop, :] for ch in chunks]

    q_dec = (q * jnp.exp(bc)).astype(BF16)
    k_dec = jnp.concatenate([k[ch] * jnp.exp(b_last[c] - bc[ch]) for c, ch in enumerate(chunks)],
                            axis=0).astype(BF16)
    o_inter = [[None] * (GLA_HEADS // 2) for _ in chunks]
    states = [st_ref[p] for p in range(GLA_HEADS // 2)]

    def recur_step(p, c):
        ch = chunks[c]
        ks = slice(p * GLA_PAIR, (p + 1) * GLA_PAIR)
        vs = slice(p * 2 * HEAD_DIM, (p + 1) * 2 * HEAD_DIM)
        st = states[p]
        o_inter[c][p] = lax.dot_general(q_dec[ch, ks], st.astype(BF16), nt, preferred_element_type=F32)
        upd = lax.dot_general(v[ch, vs], k_dec[ch, ks], tn, preferred_element_type=F32)
        states[p] = st * jnp.exp(b_last[c][:, ks]) + upd * bm_ref[...]

    off = []

    def off_step(i):
        q_parts, k_parts = [], []
        for ch in chunks:
            lo_r = ch.start + i * S
            b_row = bc[lo_r - 1:lo_r, :]
            q_parts.append(q[lo_r:lo_r + S, :] * jnp.exp(bc[lo_r:lo_r + S, :] - b_row))
            k_parts.append(k[ch] * jnp.exp(jnp.minimum(b_row - bc[ch], 0.0)))
        q_i = jnp.concatenate(q_parts, axis=0)
        k_i = jnp.concatenate(k_parts, axis=0).astype(BF16)
        lhs = jnp.concatenate([q_i * head_masks[h] for h in range(GLA_HEADS)], axis=0).astype(BF16)
        off.append(lax.dot_general(lhs, k_i, nt, preferred_element_type=F32) * om_ref[i - 1])

    def shifted(x, d):
        return pltpu.roll(x.reshape(R // S, S, GLA_KEY_WIDTH), d, 1).reshape(R, GLA_KEY_WIDTH)

    tmod = lax.broadcasted_iota(jnp.int32, (R, GLA_KEY_WIDTH), 0) % S
    diag = [(q * k).astype(BF16)]

    def diag_step(d):
        prod = q * shifted(k, d) * jnp.exp(bc - shifted(bc, d))
        diag.append(jnp.where(tmod >= d, prod, 0.0).astype(BF16))

    vector_work = ([functools.partial(off_step, i) for i in range(1, C // S)]
                   + [functools.partial(diag_step, d) for d in range(1, S)])
    chain = [functools.partial(recur_step, p, c) for c in range(GLA_NCH) for p in range(GLA_HEADS // 2)]
    emitted = 0
    for j, work in enumerate(vector_work):
        while emitted < len(chain) and emitted * len(vector_work) <= j * len(chain):
            chain[emitted]()
            emitted += 1
        work()
    for step in chain[emitted:]:
        step()
    for p in range(GLA_HEADS // 2):
        st_ref[p] = states[p]
    o = jnp.concatenate([jnp.concatenate(parts, axis=1) for parts in o_inter], axis=0)
    score = jnp.dot(jnp.concatenate(diag, axis=1), e_ref[...], preferred_element_type=F32)

    lane_head = lax.broadcasted_iota(jnp.int32, (C, LANES), 1) // S
    no_diag = jnp.zeros((C, LANES), F32)
    o_intra = []
    for h in range(GLA_HEADS):
        blocks = []
        for c, ch in enumerate(chunks):
            a_diag = pltpu.roll(jnp.where(lane_head == h, score[ch], 0.0),
                                (LANES - (S - 1) - S * h + C * c) % LANES, 1, stride=1, stride_axis=0)
            halves = [no_diag] * (R // LANES)
            halves[C * c // LANES] = a_diag
            first = (h * GLA_NCH + c) * S
            a_off = jnp.concatenate([jnp.zeros((S, R), F32)] + [a[first:first + S, :] for a in off], axis=0)
            blocks.append(a_off + jnp.concatenate(halves, axis=1))
        a_h = jnp.concatenate(blocks, axis=0).astype(BF16)
        o_intra.append(jnp.dot(a_h, v[:, h * HEAD_DIM:(h + 1) * HEAD_DIM], preferred_element_type=F32))
    o = o + jnp.concatenate(o_intra, axis=1)

    g = g_ref[...].astype(F32)
    for h in range(GLA_HEADS):
        hs = slice(h * HEAD_DIM, (h + 1) * HEAD_DIM)
        y = _rms(o[:, hs], gn_ref[:, hs])
        o_ref[:, hs] = (y * _silu(g[:, hs])).astype(o_ref.dtype)


def _gla(proj, proj_b, wup_p, bup3, tril, e_mat, bmask, offmask, out_norm3, layer):
    nr = SEQ // GLA_RB
    row = lambda b, r: b * nr + r
    return pl.pallas_call(
        _gla_kernel,
        grid=(BATCH, nr),
        in_specs=[pl.BlockSpec((GLA_RB, GLA_KEY_WIDTH), lambda b, r: (row(b, r), COL_GLA_Q // GLA_KEY_WIDTH)),
                  pl.BlockSpec((GLA_RB, GLA_KEY_WIDTH), lambda b, r: (row(b, r), COL_GLA_K // GLA_KEY_WIDTH)),
                  pl.BlockSpec((GLA_RB, GLA_WIDTH), lambda b, r: (row(b, r), COL_GLA_V // GLA_WIDTH)),
                  pl.BlockSpec((GLA_RB, LANES), lambda b, r: (row(b, r), COLB_GLA_R // LANES)),
                  pl.BlockSpec((GLA_RB, GLA_WIDTH), lambda b, r: (row(b, r), COLB_GLA_G // GLA_WIDTH)),
                  pl.BlockSpec((None, LANES, GLA_KEY_WIDTH), lambda b, r: (layer, 0, 0)),
                  pl.BlockSpec((None, 1, GLA_KEY_WIDTH), lambda b, r: (layer, 0, 0)),
                  pl.BlockSpec((GLA_RB, GLA_RB), lambda b, r: (0, 0)),
                  pl.BlockSpec((GLA_SUB * GLA_KEY_WIDTH, LANES), lambda b, r: (0, 0)),
                  pl.BlockSpec((2 * HEAD_DIM, GLA_PAIR), lambda b, r: (0, 0)),
                  pl.BlockSpec((GLA_CHUNK // GLA_SUB - 1, GLA_OFF_ROWS, GLA_RB), lambda b, r: (0, 0, 0)),
                  pl.BlockSpec((None, 1, GLA_WIDTH), lambda b, r: (layer, 0, SB_WIDTH // GLA_WIDTH))],
        out_specs=pl.BlockSpec((GLA_RB, GLA_WIDTH), lambda b, r: (row(b, r), 0)),
        out_shape=jax.ShapeDtypeStruct((TOKENS, GLA_WIDTH), BF16),
        scratch_shapes=[pltpu.VMEM((GLA_HEADS // 2, 2 * HEAD_DIM, GLA_PAIR), F32)],
        compiler_params=_params(("parallel", "arbitrary")),
        name="gla",
    )(proj, proj, proj, proj_b, proj_b, wup_p, bup3, tril, e_mat, bmask, offmask, out_norm3)


SGU_RB = 1024


def _sgu_kernel(u_ref, v_ref, g_ref, w_ref, bb_ref, sn_ref, gn_ref, o_ref):
    C = SGU_CHUNK
    keep = (lax.broadcasted_iota(jnp.int32, (C, C), 0) >= lax.broadcasted_iota(jnp.int32, (C, C), 1))
    w = [jnp.where(keep, w_ref[g], 0.0).astype(BF16) for g in range(SGU_GROUPS)]
    for c in range(SGU_RB // C):
        rows = slice(c * C, (c + 1) * C)
        u = _gelu_tanh(u_ref[rows, :].astype(F32))
        v = _gelu_tanh(v_ref[rows, :].astype(F32))
        vn = _rms(v, sn_ref[...]).astype(BF16)
        gate = g_ref[rows, :].astype(F32)
        for g in range(SGU_GROUPS):
            gs = slice(g * HEAD_DIM, (g + 1) * HEAD_DIM)
            mixed = jnp.dot(w[g], vn[:, gs], preferred_element_type=F32) + bb_ref[g]
            y = _rms(u[:, gs] * mixed, gn_ref[:, gs])
            o_ref[rows, gs] = (y * _silu(gate[:, gs])).astype(o_ref.dtype)


def _sgu(proj, w_sgu, b_sgu_b, sgu_norm3, out_norm3, layer):
    return pl.pallas_call(
        _sgu_kernel,
        grid=(TOKENS // SGU_RB,),
        in_specs=[pl.BlockSpec((SGU_RB, SGU_WIDTH), lambda i: (i, COLB_SGU_U // SGU_WIDTH)),
                  pl.BlockSpec((SGU_RB, SGU_WIDTH), lambda i: (i, COLB_SGU_V // SGU_WIDTH)),
                  pl.BlockSpec((SGU_RB, SGU_WIDTH), lambda i: (i, COLB_SGU_G // SGU_WIDTH)),
                  pl.BlockSpec((None, SGU_GROUPS, SGU_CHUNK, SGU_CHUNK), lambda i: (layer, 0, 0, 0)),
                  pl.BlockSpec((None, SGU_GROUPS, SGU_CHUNK, HEAD_DIM), lambda i: (layer, 0, 0, 0)),
                  pl.BlockSpec((None, 1, SGU_WIDTH), lambda i: (layer, 0, 0)),
                  pl.BlockSpec((None, 1, SGU_WIDTH), lambda i: (layer, 0, (SB_WIDTH + GLA_WIDTH) // SGU_WIDTH))],
        out_specs=pl.BlockSpec((SGU_RB, SGU_WIDTH), lambda i: (i, 0)),
        out_shape=jax.ShapeDtypeStruct((TOKENS, SGU_WIDTH), BF16),
        compiler_params=_params(("parallel",)),
        name="sgu",
    )(proj, proj, proj, w_sgu, b_sgu_b, sgu_norm3, out_norm3)


POST_TM = 512


def _post_kernel(x_ref, msb_ref, mgla_ref, msgu_ref, wout_ref, nx_ref, wxq_ref, kv_ref, wxo_ref, nn_ref,
                 *out_refs, last):
    mix = jnp.concatenate([msb_ref[...], mgla_ref[...], msgu_ref[...]], axis=1)
    x1 = x_ref[...] + jnp.dot(mix, wout_ref[...], preferred_element_type=F32)

    hx = _rms(x1, nx_ref[...]).astype(BF16)
    q = jnp.dot(hx, wxq_ref[...], preferred_element_type=F32).astype(BF16)
    scale = HEAD_DIM ** -0.5
    heads = []
    for h in range(XA_HEADS):
        hs = slice(h * HEAD_DIM, (h + 1) * HEAD_DIM)
        k_h = kv_ref[:, hs]
        v_h = kv_ref[:, XA_WIDTH + h * HEAD_DIM:XA_WIDTH + (h + 1) * HEAD_DIM]
        s = lax.dot_general(q[:, hs], k_h, (((1,), (1,)), ((), ())), preferred_element_type=F32) * scale
        e = jnp.exp(s - jnp.max(s, axis=-1, keepdims=True))
        p = e * (1.0 / jnp.sum(e, axis=-1, keepdims=True))
        heads.append(jnp.dot(p.astype(BF16), v_h, preferred_element_type=F32).astype(BF16))
    x2 = x1 + jnp.dot(jnp.concatenate(heads, axis=1), wxo_ref[...], preferred_element_type=F32)

    if last:
        out_refs[0][...] = _rms(x2, nn_ref[...])
    else:
        out_refs[0][...] = x2
        out_refs[1][...] = _rms(x2, nn_ref[...]).astype(BF16)


def _post(x, m_sb, m_gla, m_sgu, w_out_bf, norm_x3, w_xq_bf, kv, w_xo_bf, next_norm3, layer, next_idx, last):
    tm = POST_TM
    per_b = SEQ // tm
    row_spec = lambda width: pl.BlockSpec((tm, width), lambda i: (i, 0))
    const = lambda *shape: pl.BlockSpec((None,) + shape, lambda i: (layer,) + (0,) * len(shape),
                                        pipeline_mode=pl.Buffered(1))
    x_out = jax.ShapeDtypeStruct((TOKENS, D_MODEL), F32)
    if last:
        out_shape, out_specs = x_out, row_spec(D_MODEL)
    else:
        out_shape = (x_out, jax.ShapeDtypeStruct((TOKENS, D_MODEL), BF16))
        out_specs = (row_spec(D_MODEL), row_spec(D_MODEL))
    return pl.pallas_call(
        functools.partial(_post_kernel, last=last),
        grid=(TOKENS // tm,),
        in_specs=[row_spec(D_MODEL), row_spec(SB_WIDTH), row_spec(GLA_WIDTH), row_spec(SGU_WIDTH),
                  const(D_MODEL, D_MODEL),
                  const(1, D_MODEL),
                  const(D_MODEL, XA_WIDTH),
                  pl.BlockSpec((None, N_MEM, 2 * XA_WIDTH), lambda i: (layer, i // per_b, 0)),
                  const(XA_WIDTH, D_MODEL),
                  pl.BlockSpec((None, 1, D_MODEL), lambda i: (next_idx, 0, 0))],
        out_specs=out_specs,
        out_shape=out_shape,
        compiler_params=_params(("parallel",)),
        name="post",
    )(x, m_sb, m_gla, m_sgu, w_out_bf, norm_x3, w_xq_bf, kv, w_xo_bf, next_norm3)


def _constants():
    r = jnp.arange(2 * SB_TK)[:, None]
    c = jnp.arange(2 * SB_TK)[None, :]
    tt = jnp.where(c < SB_TK, (r % SB_TK) > c, True).astype(BF16)
    t = jnp.arange(GLA_RB)
    tril = ((t[:, None] >= t[None, :]) & (t[:, None] // GLA_CHUNK == t[None, :] // GLA_CHUNK)).astype(BF16)
    row_chunk = (jnp.arange(GLA_OFF_ROWS)[:, None] // GLA_SUB) % GLA_NCH
    offmask = jnp.stack([(row_chunk == t[None, :] // GLA_CHUNK) & (t[None, :] % GLA_CHUNK < i * GLA_SUB)
                         for i in range(1, GLA_CHUNK // GLA_SUB)]).astype(F32)
    e_row = jnp.arange(GLA_SUB * GLA_KEY_WIDTH)
    e_lane = GLA_SUB * ((e_row % GLA_KEY_WIDTH) // GLA_HEAD_K) + (GLA_SUB - 1) - e_row // GLA_KEY_WIDTH
    e_mat = (e_lane[:, None] == jnp.arange(LANES)[None, :]).astype(BF16)
    bmask = (jnp.arange(2 * HEAD_DIM)[:, None] // HEAD_DIM
             == jnp.arange(GLA_PAIR)[None, :] // GLA_HEAD_K).astype(F32)
    return tt, tril, e_mat, bmask, offmask


def kernel(x, mem, norm_mix, w_in, w_gla_gate_up, b_gla_gate, sgu_norm, w_sgu, b_sgu, out_norm, w_out,
           norm_xattn, norm_mem, w_xq, w_xkv, w_xo, final_norm):
    tt, tril, e_mat, bmask, offmask = _constants()
    w_in_t = jnp.swapaxes(w_in, 1, 2)
    wup_p = jnp.pad(w_gla_gate_up, ((0, 0), (0, LANES - GLA_GATE_RANK), (0, 0))).astype(BF16)
    w_out_bf, w_xq_bf, w_xo_bf = (w.astype(BF16) for w in (w_out, w_xq, w_xo))
    row3 = lambda a: a.reshape(a.shape[0], 1, a.shape[-1])
    norm_mix3, bup3, sgu_norm3, out_norm3 = row3(norm_mix), row3(b_gla_gate), row3(sgu_norm), row3(out_norm)
    norm_x3, norm_mem3 = row3(norm_xattn), row3(norm_mem)
    final3 = final_norm.reshape(1, 1, D_MODEL)
    b_sgu_b = jnp.broadcast_to(b_sgu[..., None], b_sgu.shape + (HEAD_DIM,))

    xf = x.reshape(TOKENS, D_MODEL)
    kv = _mem_kv(mem.reshape(BATCH * N_MEM, D_MODEL), norm_mem3, w_xkv)
    h = _norm_rows(xf, norm_mix3, 0)
    for l in range(DEPTH):
        last = l == DEPTH - 1
        proj = _in_proj_a(h, w_in_t, l)
        proj_b = _in_proj_b(h, w_in_t, l)
        m_sb = _sb_attention(proj, tt, out_norm3, l)
        m_gla = _gla(proj, proj_b, wup_p, bup3, tril, e_mat, bmask, offmask, out_norm3, l)
        m_sgu = _sgu(proj_b, w_sgu, b_sgu_b, sgu_norm3, out_norm3, l)
        nxt = final3 if last else norm_mix3
        res = _post(xf, m_sb, m_gla, m_sgu, w_out_bf, norm_x3, w_xq_bf, kv, w_xo_bf, nxt,
                    l, 0 if last else l + 1, last)
        if last:
            xf = res
        else:
            xf, h = res
    return xf.reshape(BATCH, SEQ, D_MODEL)
```

```python
import functools

import jax
import jax.numpy as jnp
from jax import lax
from jax.experimental import pallas as pl
from jax.experimental.pallas import tpu as pltpu

F32 = jnp.float32
BF16 = jnp.bfloat16

D_MODEL = 2048
BATCH = 4
SEQ = 2048
DEPTH = 4
TOKENS = BATCH * SEQ
HEAD_DIM = 128
SB_HEADS = 8
SB_WIDTH = SB_HEADS * HEAD_DIM
GLA_HEADS = 4
GLA_HEAD_K = 64
GLA_WIDTH = GLA_HEADS * HEAD_DIM
GLA_KEY_WIDTH = GLA_HEADS * GLA_HEAD_K
GLA_GATE_RANK = 16
GLA_GATE_TAU = 16.0
GLA_CHUNK = 64
GLA_SUB = 8
SGU_GROUPS = 4
SGU_WIDTH = 512
SGU_CHUNK = 128
N_MEM = 256
XA_HEADS = 4
XA_WIDTH = XA_HEADS * HEAD_DIM
EPS = 1e-6

LANES = 128

COL_SB_Q = 0
COL_SB_K = SB_WIDTH
COL_SB_V = 2 * SB_WIDTH
COL_SB_G = 3 * SB_WIDTH
COL_GLA_Q = 4 * SB_WIDTH
COL_GLA_K = COL_GLA_Q + GLA_KEY_WIDTH
COL_GLA_V = COL_GLA_K + GLA_KEY_WIDTH
PROJ_A_WIDTH = COL_GLA_V + GLA_WIDTH
COLB_GLA_G = 0
COLB_SGU_U = GLA_WIDTH
COLB_SGU_V = COLB_SGU_U + SGU_WIDTH
COLB_SGU_G = COLB_SGU_V + SGU_WIDTH
COLB_GLA_R = COLB_SGU_G + SGU_WIDTH
PROJ_B_MAIN = COLB_GLA_R
PROJ_B_WIDTH = COLB_GLA_R + LANES
ORIG_GLA_R = PROJ_A_WIDTH

VMEM_LIMIT = 56 * 1024 * 1024


def _params(semantics):
    return pltpu.CompilerParams(dimension_semantics=semantics, vmem_limit_bytes=VMEM_LIMIT)


def _rms(x, g):
    ms = jnp.mean(x * x, axis=-1, keepdims=True)
    return x * lax.rsqrt(ms + EPS) * g


def _silu(g):
    half = 0.5 * g
    return half + half * jnp.tanh(half)


def _softplus(z):
    return jnp.maximum(z, 0.0) + jnp.log(1.0 + jnp.exp(-jnp.abs(z)))


def _split_bf16(x):
    hi = x.astype(BF16)
    lo = (x - hi.astype(F32)).astype(BF16)
    return hi, lo


def _gelu_tanh(x):
    c = 0.7978845608028654
    half = 0.5 * x
    return half + half * jnp.tanh(x * (c * 0.044715 * (x * x) + c))


def _norm_kernel(x_ref, g_ref, o_ref):
    o_ref[...] = _rms(x_ref[...], g_ref[...]).astype(o_ref.dtype)


def _norm_rows(x2d, g3d, layer, tm=1024):
    rows = x2d.shape[0]
    return pl.pallas_call(
        _norm_kernel,
        grid=(rows // tm,),
        in_specs=[pl.BlockSpec((tm, D_MODEL), lambda i: (i, 0)),
                  pl.BlockSpec((None, 1, D_MODEL), lambda i: (layer, 0, 0))],
        out_specs=pl.BlockSpec((tm, D_MODEL), lambda i: (i, 0)),
        out_shape=jax.ShapeDtypeStruct((rows, D_MODEL), BF16),
        compiler_params=_params(("parallel",)),
        name="rmsnorm_rows",
    )(x2d, g3d)


def _memkv_kernel(m_ref, g_ref, w_ref, o_ref):
    mn = _rms(m_ref[...], g_ref[...]).astype(BF16)
    o_ref[...] = jnp.dot(mn, w_ref[...].astype(BF16), preferred_element_type=F32).astype(o_ref.dtype)


def _mem_kv(mem2d, norm_mem3, w_xkv):
    rows = mem2d.shape[0]
    return pl.pallas_call(
        _memkv_kernel,
        grid=(DEPTH,),
        in_specs=[pl.BlockSpec((rows, D_MODEL), lambda l: (0, 0)),
                  pl.BlockSpec((None, 1, D_MODEL), lambda l: (l, 0, 0)),
                  pl.BlockSpec((None, D_MODEL, 2 * XA_WIDTH), lambda l: (l, 0, 0))],
        out_specs=pl.BlockSpec((None, rows, 2 * XA_WIDTH), lambda l: (l, 0, 0)),
        out_shape=jax.ShapeDtypeStruct((DEPTH, rows, 2 * XA_WIDTH), BF16),
        compiler_params=_params(("parallel",)),
        name="mem_kv",
    )(mem2d, norm_mem3, w_xkv)


NT_DIMS = (((1,), (1,)), ((), ()))


def _proj_a_kernel(h_ref, wt_ref, o_ref, wbf_ref):
    @pl.when(pl.program_id(1) == 0)
    def _():
        wbf_ref[...] = wt_ref[...].astype(BF16)

    o_ref[...] = lax.dot_general(h_ref[...], wbf_ref[...], NT_DIMS,
                                 preferred_element_type=F32).astype(o_ref.dtype)


def _in_proj_a(h, w_in_t, layer, tm=1024, tn=1280):
    return pl.pallas_call(
        _proj_a_kernel,
        grid=(PROJ_A_WIDTH // tn, TOKENS // tm),
        in_specs=[pl.BlockSpec((tm, D_MODEL), lambda n, m: (m, 0)),
                  pl.BlockSpec((None, tn, D_MODEL), lambda n, m: (layer, n, 0))],
        out_specs=pl.BlockSpec((tm, tn), lambda n, m: (m, n)),
        out_shape=jax.ShapeDtypeStruct((TOKENS, PROJ_A_WIDTH), BF16),
        scratch_shapes=[pltpu.VMEM((tn, D_MODEL), BF16)],
        compiler_params=_params(("parallel", "arbitrary")),
        name="in_proj_a",
    )(h, w_in_t)


def _proj_b_kernel(h_ref, wt_ref, wr_ref, wsgu_ref, bb_ref, sn_ref, gn_ref, og_ref, or_ref, osgu_ref, wbf_ref):
    @pl.when(pl.program_id(0) == 0)
    def _():
        wbf_ref[:PROJ_B_MAIN, :] = wt_ref[...].astype(BF16)
        rank = jnp.concatenate([wr_ref[...], jnp.zeros((LANES - GLA_GATE_RANK, D_MODEL), F32)], axis=0)
        wbf_ref[PROJ_B_MAIN:, :] = rank.astype(BF16)

    res = lax.dot_general(h_ref[...], wbf_ref[...], NT_DIMS, preferred_element_type=F32)
    og_ref[...] = res[:, COLB_GLA_G:COLB_SGU_U].astype(og_ref.dtype)
    or_ref[...] = res[:, COLB_GLA_R:].astype(or_ref.dtype)
    _sgu_kernel(res[:, COLB_SGU_U:COLB_SGU_V], res[:, COLB_SGU_V:COLB_SGU_G], res[:, COLB_SGU_G:COLB_GLA_R],
                wsgu_ref, bb_ref, sn_ref, gn_ref, osgu_ref)


def _in_proj_b(h, w_in_t, w_sgu, b_sgu_b, sgu_norm3, out_norm3, layer):
    tm = SGU_RB
    once = pl.Buffered(1)
    window = lambda rows, start: pl.BlockSpec((None, pl.Element(rows), pl.Element(D_MODEL)),
                                              lambda m: (layer, start, 0), pipeline_mode=once)
    rows_out = lambda width: pl.BlockSpec((tm, width), lambda m: (m, 0))
    return pl.pallas_call(
        _proj_b_kernel,
        grid=(TOKENS // tm,),
        in_specs=[pl.BlockSpec((tm, D_MODEL), lambda m: (m, 0)),
                  window(PROJ_B_MAIN, ORIG_GLA_R + GLA_GATE_RANK),
                  window(GLA_GATE_RANK, ORIG_GLA_R),
                  pl.BlockSpec((None, SGU_GROUPS, SGU_CHUNK, SGU_CHUNK), lambda m: (layer, 0, 0, 0)),
                  pl.BlockSpec((None, SGU_GROUPS, SGU_CHUNK, HEAD_DIM), lambda m: (layer, 0, 0, 0)),
                  pl.BlockSpec((None, 1, SGU_WIDTH), lambda m: (layer, 0, 0)),
                  pl.BlockSpec((None, 1, SGU_WIDTH), lambda m: (layer, 0, (SB_WIDTH + GLA_WIDTH) // SGU_WIDTH))],
        out_specs=(rows_out(GLA_WIDTH), rows_out(LANES), rows_out(SGU_WIDTH)),
        out_shape=(jax.ShapeDtypeStruct((TOKENS, GLA_WIDTH), BF16),
                   jax.ShapeDtypeStruct((TOKENS, LANES), BF16),
                   jax.ShapeDtypeStruct((TOKENS, SGU_WIDTH), BF16)),
        scratch_shapes=[pltpu.VMEM((PROJ_B_WIDTH, D_MODEL), BF16)],
        compiler_params=_params(("arbitrary",)),
        name="in_proj_b",
    )(h, w_in_t, w_in_t, w_sgu, b_sgu_b, sgu_norm3, out_norm3)


SB_TQ = 256
SB_TK = 128
SB_NQ = SEQ // SB_TQ
SB_QB = SB_TQ // SB_TK
SB_DONE = -88.0
SB_HG = 4
SB_GW = SB_HG * HEAD_DIM


def _sb_suffix(log_keep, tt):
    hi, lo = _split_bf16(log_keep)
    return jnp.dot(jnp.concatenate([hi, lo], axis=1), tt, preferred_element_type=F32)


def _sb_own_rows(z, v, tt, causal):
    sp = _softplus(z)
    log_keep = jnp.where(causal, -sp, 0.0)
    cs_r = _sb_suffix(log_keep[:, SB_TK:], tt)
    cs_l = _sb_suffix(log_keep[:, :SB_TK], tt)
    log_beta = z - sp
    carry_l = cs_r[:, SB_TK:]
    a = jnp.exp(jnp.concatenate([log_beta[:, :SB_TK] + cs_l[:, :SB_TK] + carry_l,
                                 log_beta[:, SB_TK:] + cs_r[:, :SB_TK]], axis=1))
    a = jnp.where(causal, a, 0.0)
    return jnp.dot(a.astype(BF16), v, preferred_element_type=F32), carry_l + cs_l[:, SB_TK:]


def _sb_block(z, tt, carry):
    sp = _softplus(z)
    cs = _sb_suffix(-sp, tt)
    return jnp.exp((z - sp) + cs[:, :SB_TK] + carry), carry + cs[:, SB_TK:]


def _sb_kernel(q_ref, k_ref, v_ref, g_ref, tt_ref, gn_ref, o_ref, acc_ref, carry_ref, zd_ref, z1_ref):
    scale = HEAD_DIM ** -0.5
    nt = (((1,), (1,)), ((), ()))
    heads = [slice(i * HEAD_DIM, (i + 1) * HEAD_DIM) for i in range(SB_HG)]
    causal = (lax.broadcasted_iota(jnp.int32, (SB_TQ, SB_TQ), 1)
              < lax.broadcasted_iota(jnp.int32, (SB_TQ, SB_TQ), 0))

    def key_rows(first_block, n_blocks):
        if isinstance(first_block, int):
            return slice(first_block * SB_TK, (first_block + n_blocks) * SB_TK)
        return pl.ds(pl.multiple_of(first_block * SB_TK, SB_TK), n_blocks * SB_TK)

    def tile_rows(t):
        return key_rows(t * SB_QB, SB_QB)

    def logits(i, qt, first_block, n_blocks):
        return lax.dot_general(q_ref[tile_rows(qt), heads[i]], k_ref[key_rows(first_block, n_blocks), heads[i]],
                               nt, preferred_element_type=F32) * scale

    def finish(qi, which):
        for i in which:
            y = _rms(acc_ref[i], gn_ref[:, heads[i]])
            gate = g_ref[tile_rows(qi), heads[i]].astype(F32)
            o_ref[tile_rows(qi), heads[i]] = (y * _silu(gate)).astype(o_ref.dtype)

    def prefetch_next_tile(i, qi):
        nxt = jnp.minimum(qi + 1, SB_NQ - 1)
        zd_ref[i] = logits(i, nxt, nxt * SB_QB, SB_QB)
        z1_ref[i] = logits(i, nxt, nxt * SB_QB - 1, 1)

    def tile(qi, c):
        worst = []
        for i in range(SB_HG):
            zd, z1 = zd_ref[i], z1_ref[i]
            prefetch_next_tile(i, qi)
            pv, carry = _sb_own_rows(zd, v_ref[tile_rows(qi), heads[i]], tt_ref[...], causal)
            a, carry = _sb_block(z1, tt_ref[...], carry)
            acc_ref[i] = pv + jnp.dot(a.astype(BF16), v_ref[key_rows(qi * SB_QB - 1, 1), heads[i]],
                                      preferred_element_type=F32)
            carry_ref[i] = carry
            worst.append(jnp.max(carry))

        def more(state):
            kb, *left = state
            return jnp.logical_and(kb >= 0, functools.reduce(jnp.maximum, left) > SB_DONE)

        def block_step(state):
            kb, *left = state

            def head_step(i):
                a, carry = _sb_block(logits(i, qi, kb, 1), tt_ref[...], carry_ref[i])
                acc_ref[i] = acc_ref[i] + jnp.dot(a.astype(BF16), v_ref[key_rows(kb, 1), heads[i]],
                                                  preferred_element_type=F32)
                carry_ref[i] = carry
                return jnp.max(carry)

            return (kb - 1, *[lax.cond(left[i] > SB_DONE, functools.partial(head_step, i), lambda i=i: left[i])
                              for i in range(SB_HG)])

        finish(qi, range(SB_HG))
        lax.while_loop(more, block_step, (qi * SB_QB - 2, *worst))
        for i in range(SB_HG):
            @pl.when(worst[i] > SB_DONE)
            def _(i=i):
                finish(qi, [i])

        return c

    for i in range(SB_HG):
        pv, _ = _sb_own_rows(logits(i, 0, 0, SB_QB), v_ref[0:SB_TQ, heads[i]], tt_ref[...], causal)
        acc_ref[i] = pv
        prefetch_next_tile(i, 0)
    finish(0, range(SB_HG))
    lax.fori_loop(1, SB_NQ, tile, 0)


def _sb_attention(proj, tt, out_norm3, layer):
    group_cols = lambda first: pl.BlockSpec((SEQ, SB_GW), lambda b, h: (b, first // SB_GW + h))
    return pl.pallas_call(
        _sb_kernel,
        grid=(BATCH, SB_HEADS // SB_HG),
        in_specs=[group_cols(COL_SB_Q), group_cols(COL_SB_K), group_cols(COL_SB_V), group_cols(COL_SB_G),
                  pl.BlockSpec((2 * SB_TK, 2 * SB_TK), lambda b, h: (0, 0)),
                  pl.BlockSpec((None, 1, SB_GW), lambda b, h: (layer, 0, h))],
        out_specs=group_cols(0),
        out_shape=jax.ShapeDtypeStruct((TOKENS, SB_WIDTH), BF16),
        scratch_shapes=[pltpu.VMEM((SB_HG, SB_TQ, HEAD_DIM), F32), pltpu.VMEM((SB_HG, SB_TQ, SB_TK), F32),
                        pltpu.VMEM((SB_HG, SB_TQ, SB_TQ), F32), pltpu.VMEM((SB_HG, SB_TQ, SB_TK), F32)],
        compiler_params=_params(("parallel", "parallel")),
        name="sb_attention",
    )(proj, proj, proj, proj, tt, out_norm3)


GLA_RB = 256
GLA_NCH = GLA_RB // GLA_CHUNK
GLA_OFF_ROWS = GLA_HEADS * GLA_NCH * GLA_SUB
GLA_PAIR = 2 * GLA_HEAD_K


def _gla_kernel(q_ref, k_ref, v_ref, r_ref, g_ref, wup_ref, bup_ref, tril_ref, e_ref, bm_ref, om_ref, gn_ref,
                o_ref, st_ref):
    @pl.when(pl.program_id(1) == 0)
    def _():
        st_ref[...] = jnp.zeros_like(st_ref)

    R, C, S = GLA_RB, GLA_CHUNK, GLA_SUB
    chunks = [slice(c * C, (c + 1) * C) for c in range(GLA_NCH)]
    lane_k = lax.broadcasted_iota(jnp.int32, (1, GLA_KEY_WIDTH), 1)
    head_masks = [(lane_k // GLA_HEAD_K == h).astype(F32) for h in range(GLA_HEADS)]
    nt = (((1,), (1,)), ((), ()))
    tn = (((0,), (0,)), ((), ()))

    q = q_ref[...].astype(F32) * (GLA_HEAD_K ** -0.5)
    k = k_ref[...].astype(F32)
    v = v_ref[...]
    logits = jnp.dot(r_ref[...], wup_ref[...], preferred_element_type=F32) + bup_ref[...]
    log_alpha = (jnp.minimum(logits, 0.0) - jnp.log(1.0 + jnp.exp(-jnp.abs(logits)))) * (1.0 / GLA_GATE_TAU)
    hi, lo = _split_bf16(log_alpha)
    tril = tril_ref[...]
    bc = (jnp.dot(tril, hi, preferred_element_type=F32)
          + jnp.dot(tril, lo, preferred_element_type=F32))
    b_last = [bc[ch.stop - 1:ch.stop, :] for ch in chunks]

    q_dec = (q * jnp.exp(bc)).astype(BF16)
    k_dec = jnp.concatenate([k[ch] * jnp.exp(b_last[c] - bc[ch]) for c, ch in enumerate(chunks)],
                            axis=0).astype(BF16)
    o_inter = [[None] * (GLA_HEADS // 2) for _ in chunks]
    states = [st_ref[p] for p in range(GLA_HEADS // 2)]

    def recur_step(p, c):
        ch = chunks[c]
        ks = slice(p * GLA_PAIR, (p + 1) * GLA_PAIR)
        vs = slice(p * 2 * HEAD_DIM, (p + 1) * 2 * HEAD_DIM)
        st = states[p]
        o_inter[c][p] = lax.dot_general(q_dec[ch, ks], st.astype(BF16), nt, preferred_element_type=F32)
        upd = lax.dot_general(v[ch, vs], k_dec[ch, ks], tn, preferred_element_type=F32)
        states[p] = st * jnp.exp(b_last[c][:, ks]) + upd * bm_ref[...]

    off = []

    def off_step(i):
        q_parts, k_parts = [], []
        for ch in chunks:
            lo_r = ch.start + i * S
            b_row = bc[lo_r - 1:lo_r, :]
            q_parts.append(q[lo_r:lo_r + S, :] * jnp.exp(bc[lo_r:lo_r + S, :] - b_row))
            k_parts.append(k[ch] * jnp.exp(jnp.minimum(b_row - bc[ch], 0.0)))
        q_i = jnp.concatenate(q_parts, axis=0)
        k_i = jnp.concatenate(k_parts, axis=0).astype(BF16)
        lhs = jnp.concatenate([q_i * head_masks[h] for h in range(GLA_HEADS)], axis=0).astype(BF16)
        off.append(lax.dot_general(lhs, k_i, nt, preferred_element_type=F32) * om_ref[i - 1])

    def shifted(x, d):
        return pltpu.roll(x.reshape(R // S, S, GLA_KEY_WIDTH), d, 1).reshape(R, GLA_KEY_WIDTH)

    tmod = lax.broadcasted_iota(jnp.int32, (R, GLA_KEY_WIDTH), 0) % S
    diag = [(q * k).astype(BF16)]

    def diag_step(d):
        prod = q * shifted(k, d) * jnp.exp(bc - shifted(bc, d))
        diag.append(jnp.where(tmod >= d, prod, 0.0).astype(BF16))

    vector_work = ([functools.partial(off_step, i) for i in range(1, C // S)]
                   + [functools.partial(diag_step, d) for d in range(1, S)])
    chain = [functools.partial(recur_step, p, c) for c in range(GLA_NCH) for p in range(GLA_HEADS // 2)]
    emitted = 0
    for j, work in enumerate(vector_work):
        while emitted < len(chain) and emitted * len(vector_work) <= j * len(chain):
            chain[emitted]()
            emitted += 1
        work()
    for step in chain[emitted:]:
        step()
    for p in range(GLA_HEADS // 2):
        st_ref[p] = states[p]
    o = jnp.concatenate([jnp.concatenate(parts, axis=1) for parts in o_inter], axis=0)
    score = jnp.dot(jnp.concatenate(diag, axis=1), e_ref[...], preferred_element_type=F32)

    lane_head = lax.broadcasted_iota(jnp.int32, (C, LANES), 1) // S
    no_diag = jnp.zeros((C, LANES), F32)
    o_intra = []
    for h in range(GLA_HEADS):
        blocks = []
        for c, ch in enumerate(chunks):
            a_diag = pltpu.roll(jnp.where(lane_head == h, score[ch], 0.0),
                                (LANES - (S - 1) - S * h + C * c) % LANES, 1, stride=1, stride_axis=0)
            halves = [no_diag] * (R // LANES)
            halves[C * c // LANES] = a_diag
            first = (h * GLA_NCH + c) * S
            a_off = jnp.concatenate([jnp.zeros((S, R), F32)] + [a[first:first + S, :] for a in off], axis=0)
            blocks.append(a_off + jnp.concatenate(halves, axis=1))
        a_h = jnp.concatenate(blocks, axis=0).astype(BF16)
        o_intra.append(jnp.dot(a_h, v[:, h * HEAD_DIM:(h + 1) * HEAD_DIM], preferred_element_type=F32))
    o = o + jnp.concatenate(o_intra, axis=1)

    g = g_ref[...].astype(F32)
    for h in range(GLA_HEADS):
        hs = slice(h * HEAD_DIM, (h + 1) * HEAD_DIM)
        y = _rms(o[:, hs], gn_ref[:, hs])
        o_ref[:, hs] = (y * _silu(g[:, hs])).astype(o_ref.dtype)


def _gla(proj, proj_g, proj_r, wup_p, bup3, tril, e_mat, bmask, offmask, out_norm3, layer):
    nr = SEQ // GLA_RB
    row = lambda b, r: b * nr + r
    return pl.pallas_call(
        _gla_kernel,
        grid=(BATCH, nr),
        in_specs=[pl.BlockSpec((GLA_RB, GLA_KEY_WIDTH), lambda b, r: (row(b, r), COL_GLA_Q // GLA_KEY_WIDTH)),
                  pl.BlockSpec((GLA_RB, GLA_KEY_WIDTH), lambda b, r: (row(b, r), COL_GLA_K // GLA_KEY_WIDTH)),
                  pl.BlockSpec((GLA_RB, GLA_WIDTH), lambda b, r: (row(b, r), COL_GLA_V // GLA_WIDTH)),
                  pl.BlockSpec((GLA_RB, LANES), lambda b, r: (row(b, r), 0)),
                  pl.BlockSpec((GLA_RB, GLA_WIDTH), lambda b, r: (row(b, r), 0)),
                  pl.BlockSpec((None, LANES, GLA_KEY_WIDTH), lambda b, r: (layer, 0, 0)),
                  pl.BlockSpec((None, 1, GLA_KEY_WIDTH), lambda b, r: (layer, 0, 0)),
                  pl.BlockSpec((GLA_RB, GLA_RB), lambda b, r: (0, 0)),
                  pl.BlockSpec((GLA_SUB * GLA_KEY_WIDTH, LANES), lambda b, r: (0, 0)),
                  pl.BlockSpec((2 * HEAD_DIM, GLA_PAIR), lambda b, r: (0, 0)),
                  pl.BlockSpec((GLA_CHUNK // GLA_SUB - 1, GLA_OFF_ROWS, GLA_RB), lambda b, r: (0, 0, 0)),
                  pl.BlockSpec((None, 1, GLA_WIDTH), lambda b, r: (layer, 0, SB_WIDTH // GLA_WIDTH))],
        out_specs=pl.BlockSpec((GLA_RB, GLA_WIDTH), lambda b, r: (row(b, r), 0)),
        out_shape=jax.ShapeDtypeStruct((TOKENS, GLA_WIDTH), BF16),
        scratch_shapes=[pltpu.VMEM((GLA_HEADS // 2, 2 * HEAD_DIM, GLA_PAIR), F32)],
        compiler_params=_params(("parallel", "arbitrary")),
        name="gla",
    )(proj, proj, proj, proj_r, proj_g, wup_p, bup3, tril, e_mat, bmask, offmask, out_norm3)


SGU_RB = 1024


def _sgu_kernel(u_ref, v_ref, g_ref, w_ref, bb_ref, sn_ref, gn_ref, o_ref):
    C = SGU_CHUNK
    keep = (lax.broadcasted_iota(jnp.int32, (C, C), 0) >= lax.broadcasted_iota(jnp.int32, (C, C), 1))
    w = [jnp.where(keep, w_ref[g], 0.0).astype(BF16) for g in range(SGU_GROUPS)]
    for c in range(SGU_RB // C):
        rows = slice(c * C, (c + 1) * C)
        u = _gelu_tanh(u_ref[rows, :].astype(F32))
        v = _gelu_tanh(v_ref[rows, :].astype(F32))
        vn = _rms(v, sn_ref[...]).astype(BF16)
        gate = g_ref[rows, :].astype(F32)
        for g in range(SGU_GROUPS):
            gs = slice(g * HEAD_DIM, (g + 1) * HEAD_DIM)
            mixed = jnp.dot(w[g], vn[:, gs], preferred_element_type=F32) + bb_ref[g]
            y = _rms(u[:, gs] * mixed, gn_ref[:, gs])
            o_ref[rows, gs] = (y * _silu(gate[:, gs])).astype(o_ref.dtype)


POST_TM = 512


def _post_kernel(x_ref, msb_ref, mgla_ref, msgu_ref, wout_ref, nx_ref, wxq_ref, kv_ref, wxo_ref, nn_ref,
                 *out_refs, last):
    mix = jnp.concatenate([msb_ref[...], mgla_ref[...], msgu_ref[...]], axis=1)
    x1 = x_ref[...] + jnp.dot(mix, wout_ref[...], preferred_element_type=F32)

    hx = _rms(x1, nx_ref[...]).astype(BF16)
    q = jnp.dot(hx, wxq_ref[...], preferred_element_type=F32).astype(BF16)
    scale = HEAD_DIM ** -0.5
    heads = []
    for h in range(XA_HEADS):
        hs = slice(h * HEAD_DIM, (h + 1) * HEAD_DIM)
        k_h = kv_ref[:, hs]
        v_h = kv_ref[:, XA_WIDTH + h * HEAD_DIM:XA_WIDTH + (h + 1) * HEAD_DIM]
        s = lax.dot_general(q[:, hs], k_h, (((1,), (1,)), ((), ())), preferred_element_type=F32) * scale
        e = jnp.exp(s - jnp.max(s, axis=-1, keepdims=True))
        p = e * (1.0 / jnp.sum(e, axis=-1, keepdims=True))
        heads.append(jnp.dot(p.astype(BF16), v_h, preferred_element_type=F32).astype(BF16))
    x2 = x1 + jnp.dot(jnp.concatenate(heads, axis=1), wxo_ref[...], preferred_element_type=F32)

    if last:
        out_refs[0][...] = _rms(x2, nn_ref[...])
    else:
        out_refs[0][...] = x2
        out_refs[1][...] = _rms(x2, nn_ref[...]).astype(BF16)


def _post(x, m_sb, m_gla, m_sgu, w_out_bf, norm_x3, w_xq_bf, kv, w_xo_bf, next_norm3, layer, next_idx, last):
    tm = POST_TM
    per_b = SEQ // tm
    row_spec = lambda width: pl.BlockSpec((tm, width), lambda i: (i, 0))
    const = lambda *shape: pl.BlockSpec((None,) + shape, lambda i: (layer,) + (0,) * len(shape),
                                        pipeline_mode=pl.Buffered(1))
    x_out = jax.ShapeDtypeStruct((TOKENS, D_MODEL), F32)
    if last:
        out_shape, out_specs = x_out, row_spec(D_MODEL)
    else:
        out_shape = (x_out, jax.ShapeDtypeStruct((TOKENS, D_MODEL), BF16))
        out_specs = (row_spec(D_MODEL), row_spec(D_MODEL))
    return pl.pallas_call(
        functools.partial(_post_kernel, last=last),
        grid=(TOKENS // tm,),
        in_specs=[row_spec(D_MODEL), row_spec(SB_WIDTH), row_spec(GLA_WIDTH), row_spec(SGU_WIDTH),
                  const(D_MODEL, D_MODEL),
                  const(1, D_MODEL),
                  const(D_MODEL, XA_WIDTH),
                  pl.BlockSpec((None, N_MEM, 2 * XA_WIDTH), lambda i: (layer, i // per_b, 0)),
                  const(XA_WIDTH, D_MODEL),
                  pl.BlockSpec((None, 1, D_MODEL), lambda i: (next_idx, 0, 0))],
        out_specs=out_specs,
        out_shape=out_shape,
        compiler_params=_params(("parallel",)),
        name="post",
    )(x, m_sb, m_gla, m_sgu, w_out_bf, norm_x3, w_xq_bf, kv, w_xo_bf, next_norm3)


def _constants():
    r = jnp.arange(2 * SB_TK)[:, None]
    c = jnp.arange(2 * SB_TK)[None, :]
    tt = jnp.where(c < SB_TK, (r % SB_TK) > c, True).astype(BF16)
    t = jnp.arange(GLA_RB)
    tril = ((t[:, None] >= t[None, :]) & (t[:, None] // GLA_CHUNK == t[None, :] // GLA_CHUNK)).astype(BF16)
    row_chunk = (jnp.arange(GLA_OFF_ROWS)[:, None] // GLA_SUB) % GLA_NCH
    offmask = jnp.stack([(row_chunk == t[None, :] // GLA_CHUNK) & (t[None, :] % GLA_CHUNK < i * GLA_SUB)
                         for i in range(1, GLA_CHUNK // GLA_SUB)]).astype(F32)
    e_row = jnp.arange(GLA_SUB * GLA_KEY_WIDTH)
    e_lane = GLA_SUB * ((e_row % GLA_KEY_WIDTH) // GLA_HEAD_K) + (GLA_SUB - 1) - e_row // GLA_KEY_WIDTH
    e_mat = (e_lane[:, None] == jnp.arange(LANES)[None, :]).astype(BF16)
    bmask = (jnp.arange(2 * HEAD_DIM)[:, None] // HEAD_DIM
             == jnp.arange(GLA_PAIR)[None, :] // GLA_HEAD_K).astype(F32)
    return tt, tril, e_mat, bmask, offmask


def kernel(x, mem, norm_mix, w_in, w_gla_gate_up, b_gla_gate, sgu_norm, w_sgu, b_sgu, out_norm, w_out,
           norm_xattn, norm_mem, w_xq, w_xkv, w_xo, final_norm):
    tt, tril, e_mat, bmask, offmask = _constants()
    w_in_t = jnp.swapaxes(w_in, 1, 2)
    wup_p = jnp.pad(w_gla_gate_up, ((0, 0), (0, LANES - GLA_GATE_RANK), (0, 0))).astype(BF16)
    w_out_bf, w_xq_bf, w_xo_bf = (w.astype(BF16) for w in (w_out, w_xq, w_xo))
    row3 = lambda a: a.reshape(a.shape[0], 1, a.shape[-1])
    norm_mix3, bup3, sgu_norm3, out_norm3 = row3(norm_mix), row3(b_gla_gate), row3(sgu_norm), row3(out_norm)
    norm_x3, norm_mem3 = row3(norm_xattn), row3(norm_mem)
    final3 = final_norm.reshape(1, 1, D_MODEL)
    b_sgu_b = jnp.broadcast_to(b_sgu[..., None], b_sgu.shape + (HEAD_DIM,))

    xf = x.reshape(TOKENS, D_MODEL)
    kv = _mem_kv(mem.reshape(BATCH * N_MEM, D_MODEL), norm_mem3, w_xkv)
    h = _norm_rows(xf, norm_mix3, 0)
    for l in range(DEPTH):
        last = l == DEPTH - 1
        proj = _in_proj_a(h, w_in_t, l)
        proj_g, proj_r, m_sgu = _in_proj_b(h, w_in_t, w_sgu, b_sgu_b, sgu_norm3, out_norm3, l)
        m_sb = _sb_attention(proj, tt, out_norm3, l)
        m_gla = _gla(proj, proj_g, proj_r, wup_p, bup3, tril, e_mat, bmask, offmask, out_norm3, l)
        nxt = final3 if last else norm_mix3
        res = _post(xf, m_sb, m_gla, m_sgu, w_out_bf, norm_x3, w_xq_bf, kv, w_xo_bf, nxt,
                    l, 0 if last else l + 1, last)
        if last:
            xf = res
        else:
            xf, h = res
    return xf.reshape(BATCH, SEQ, D_MODEL)
```
